```python
import math
import jax, jax.numpy as jnp
from jax import lax
import numpy as np


D_MODEL = 1024
BATCH = 8
SEQ = 2048
DEPTH = 1
DEC_BATCH = 128
DEC_SEQ = 8
PAST_LEN = 16384
PAGE_SIZE = 128

PLE_DIM = 256
D_LRU = D_MODEL
LRU_BLOCKS = 16
LRU_BLOCK = D_LRU // LRU_BLOCKS
LRU_CONV = 4
LRU_C = 8.0
RET_HEADS = 8
RET_DK = 64
RET_DV = 128
RET_CHUNK = 128
ROPE_BASE = 10000.0
D_FF = 3 * D_MODEL
FFN_CONV = 3
EPS = 1e-6
IN_SPLITS = (D_LRU, D_LRU, RET_HEADS * RET_DK, RET_HEADS * RET_DK,
             RET_HEADS * RET_DV, RET_HEADS * RET_DV, D_MODEL, D_MODEL)
D_IN = 2 * D_LRU + 2 * RET_HEADS * RET_DK + 2 * RET_HEADS * RET_DV + 2 * D_MODEL

kernel_name = 'hybrid_rglru_retention_convffn_step'


def _rmsnorm(x, g):
    xf = x.astype(jnp.float32)
    y = xf * lax.rsqrt(jnp.mean(xf * xf, axis=-1, keepdims=True) + EPS)
    return (y * g.astype(jnp.float32)).astype(x.dtype)


def _causal_dwconv(x, buf, w, b):
    width = w.shape[0]
    L = x.shape[1]
    xc = jnp.concatenate([buf.astype(x.dtype), x], axis=1)
    y = sum((xc[:, j:j + L] * w[j] for j in range(width)), b.astype(x.dtype))
    return y, xc[:, xc.shape[1] - (width - 1):]


def _rglru(x, h0, w_r, b_r, w_i, b_i, lam):
    B, L, C = x.shape
    xb = x.reshape(B, L, LRU_BLOCKS, LRU_BLOCK)
    r = jax.nn.sigmoid((jnp.einsum('blgi,gij->blgj', xb, w_r).reshape(B, L, C) + b_r).astype(jnp.float32))
    ig = jax.nn.sigmoid((jnp.einsum('blgi,gij->blgj', xb, w_i).reshape(B, L, C) + b_i).astype(jnp.float32))
    log_a = -LRU_C * r * jax.nn.softplus(-lam.astype(jnp.float32))
    a = jnp.exp(log_a)
    u = jnp.sqrt(-jnp.expm1(2.0 * log_a)) * (ig * x.astype(jnp.float32))

    def combine(lhs, rhs):
        a1, b1 = lhs
        a2, b2 = rhs
        return a1 * a2, a2 * b1 + b2

    a_cum, b_cum = lax.associative_scan(combine, (a, u), axis=1)
    h = a_cum * h0.astype(jnp.float32)[:, None] + b_cum
    return h.astype(x.dtype), h[:, -1].astype(h0.dtype)


def _rotary(x, pos):
    half = x.shape[-1] // 2
    inv = ROPE_BASE ** (-jnp.arange(half, dtype=jnp.float32) / half)
    ang = pos.astype(jnp.float32)[:, None] * inv[None, :]
    cos = jnp.cos(ang)[None, :, None, :]
    sin = jnp.sin(ang)[None, :, None, :]
    xf = x.astype(jnp.float32)
    x1, x2 = xf[..., :half], xf[..., half:]
    return jnp.concatenate([x1 * cos - x2 * sin, x2 * cos + x1 * sin], axis=-1)


def _retention(q, k, v, s0):
    B, L, H, DK = q.shape
    DV = v.shape[-1]
    C = math.gcd(L, RET_CHUNK)
    n = L // C
    log_gamma = jnp.log1p(-(2.0 ** (-5.0 - jnp.arange(H, dtype=jnp.float32))))
    idx = jnp.arange(C)
    rel = idx[:, None] - idx[None, :]
    decay_mask = jnp.where(rel[None] >= 0,
                           jnp.exp(log_gamma[:, None, None] * jnp.maximum(rel, 0).astype(jnp.float32)[None]),
                           0.0)
    q_decay = jnp.exp(log_gamma[None, :] * (idx + 1).astype(jnp.float32)[:, None])[None, :, :, None]
    k_decay = jnp.exp(log_gamma[None, :] * (C - 1 - idx).astype(jnp.float32)[:, None])[None, :, :, None]
    chunk_decay = jnp.exp(log_gamma * C)[None, :, None, None]

    def to_chunks(t):
        return jnp.swapaxes(t.astype(jnp.float32).reshape(B, n, C, H, t.shape[-1]), 0, 1)

    def step(s, blk):
        qc, kc, vc = blk
        scores = jnp.einsum('bihd,bjhd->bhij', qc, kc) * decay_mask[None]
        o = jnp.einsum('bhij,bjhe->bihe', scores, vc)
        o = o + jnp.einsum('bihd,bhde->bihe', qc, s) * q_decay
        s_new = s * chunk_decay + jnp.einsum('bjhd,bjhe->bhde', kc * k_decay, vc)
        return s_new, o

    s_fin, o = lax.scan(step, s0.astype(jnp.float32), (to_chunks(q), to_chunks(k), to_chunks(v)))
    o = jnp.swapaxes(o, 0, 1).reshape(B, L, H, DV)
    return o, s_fin.astype(s0.dtype)


def _head_norm(o, g, b):
    B, L, H, DV = o.shape
    mu = jnp.mean(o, axis=-1, keepdims=True)
    var = jnp.mean(jnp.square(o - mu), axis=-1, keepdims=True)
    y = ((o - mu) * lax.rsqrt(var + EPS)).reshape(B, L, H * DV)
    return y * g.astype(jnp.float32) + b.astype(jnp.float32)


def _split_cols(proj):
    parts = []
    off = 0
    for size in IN_SPLITS:
        parts.append(proj[..., off:off + size])
        off += size
    return parts


def _layer(x, p, pos, conv_lru0, h_lru0, s_ret0, conv_ffn0,
           g_mix, w_in, w_lru_conv, b_lru_conv, w_r, b_r, w_i, b_i, lru_lambda, w_lru_out,
           gn_g, gn_b, w_ret_out, w_o, g_ffn, w_up, w_ffn_conv, b_ffn_conv, w_down,
           w_ple, g_ple, w_ple_gate):
    B, L, _ = x.shape
    nx = _rmsnorm(x, g_mix)
    proj = nx @ w_in
    lx, lg, q, k, v, rg, ga, gb = _split_cols(proj)
    xc, conv_lru_new = _causal_dwconv(lx, conv_lru0, w_lru_conv, b_lru_conv)
    hs, h_lru_new = _rglru(xc, h_lru0, w_r, b_r, w_i, b_i, lru_lambda)
    ya = (hs * jax.nn.gelu(lg)) @ w_lru_out
    qh = _rotary(q.reshape(B, L, RET_HEADS, RET_DK), pos)
    kh = _rotary(k.reshape(B, L, RET_HEADS, RET_DK), pos) * (RET_DK ** -0.5)
    o, s_ret_new = _retention(qh, kh, v.reshape(B, L, RET_HEADS, RET_DV), s_ret0)
    o = _head_norm(o, gn_g, gn_b).astype(x.dtype)
    yb = (o * jax.nn.silu(rg)) @ w_ret_out
    merged = jax.nn.sigmoid(ga) * ya + jax.nn.sigmoid(gb) * yb
    x = x + merged @ w_o
    up = _rmsnorm(x, g_ffn) @ w_up
    upc, conv_ffn_new = _causal_dwconv(up, conv_ffn0, w_ffn_conv, b_ffn_conv)
    gate, val = upc[..., :D_FF], upc[..., D_FF:]
    x = x + (jax.nn.gelu(gate) * val) @ w_down
    e = _rmsnorm(p.astype(x.dtype) @ w_ple, g_ple)
    x = x + jax.nn.sigmoid(x @ w_ple_gate) * e
    return x, conv_lru_new, h_lru_new, s_ret_new, conv_ffn_new


def _run_group(x, p, pos, st_conv_lru, st_h, st_ret, st_conv_ffn, weights, g_final):
    cl, hl, rl, cf = [], [], [], []
    for i in range(DEPTH):
        lw = tuple(w[i] for w in weights)
        x, c1, h1, r1, f1 = _layer(x, p[i], pos, st_conv_lru[i], st_h[i], st_ret[i], st_conv_ffn[i], *lw)
        cl.append(c1)
        hl.append(h1)
        rl.append(r1)
        cf.append(f1)
    y = _rmsnorm(x, g_final)
    return y, jnp.stack(cl), jnp.stack(hl), jnp.stack(rl), jnp.stack(cf)


def setup_inputs(seed: int = 0) -> dict:
    key = jax.random.key(seed)
    ks = jax.random.split(key, 32)
    f32 = jnp.float32

    def nrm(k, shape, scale):
        return jax.random.normal(k, shape, f32) * scale

    H_DV = RET_HEADS * RET_DV
    u = jax.random.uniform(ks[16], (DEPTH, D_LRU), f32, 0.9, 0.999)
    a_base = u ** (1.0 / LRU_C)
    lru_lambda = jnp.log(a_base) - jnp.log1p(-a_base)
    return {
        'x_prompt': nrm(ks[0], (BATCH, SEQ, D_MODEL), 1.0),
        'x_sample': nrm(ks[1], (DEC_BATCH, DEC_SEQ, D_MODEL), 1.0),
        'p_prompt': nrm(ks[2], (DEPTH, BATCH, SEQ, PLE_DIM), 1.0),
        'p_sample': nrm(ks[3], (DEPTH, DEC_BATCH, DEC_SEQ, PLE_DIM), 1.0),
        'state_lru_conv': nrm(ks[4], (DEPTH, DEC_BATCH, LRU_CONV - 1, D_LRU), 1.0),
        'state_lru_h': nrm(ks[5], (DEPTH, DEC_BATCH, D_LRU), 0.5),
        'state_ret': nrm(ks[6], (DEPTH, DEC_BATCH, RET_HEADS, RET_DK, RET_DV), 0.5),
        'state_ffn_conv': nrm(ks[7], (DEPTH, DEC_BATCH, FFN_CONV - 1, 2 * D_FF), 1.0),
        'g_mix': 1.0 + nrm(ks[8], (DEPTH, D_MODEL), 0.05),
        'w_in': nrm(ks[9], (DEPTH, D_MODEL, D_IN), D_MODEL ** -0.5),
        'w_lru_conv': nrm(ks[10], (DEPTH, LRU_CONV, D_LRU), LRU_CONV ** -0.5),
        'b_lru_conv': nrm(ks[11], (DEPTH, D_LRU), 0.02),
        'w_r': nrm(ks[12], (DEPTH, LRU_BLOCKS, LRU_BLOCK, LRU_BLOCK), LRU_BLOCK ** -0.5),
        'b_r': nrm(ks[13], (DEPTH, D_LRU), 0.02),
        'w_i': nrm(ks[14], (DEPTH, LRU_BLOCKS, LRU_BLOCK, LRU_BLOCK), LRU_BLOCK ** -0.5),
        'b_i': nrm(ks[15], (DEPTH, D_LRU), 0.02),
        'lru_lambda': lru_lambda,
        'w_lru_out': nrm(ks[17], (DEPTH, D_LRU, D_MODEL), D_LRU ** -0.5),
        'gn_g': 1.0 + nrm(ks[18], (DEPTH, H_DV), 0.05),
        'gn_b': nrm(ks[19], (DEPTH, H_DV), 0.02),
        'w_ret_out': nrm(ks[20], (DEPTH, H_DV, D_MODEL), H_DV ** -0.5),
        'w_o': nrm(ks[21], (DEPTH, D_MODEL, D_MODEL), D_MODEL ** -0.5),
        'g_ffn': 1.0 + nrm(ks[22], (DEPTH, D_MODEL), 0.05),
        'w_up': nrm(ks[23], (DEPTH, D_MODEL, 2 * D_FF), D_MODEL ** -0.5),
        'w_ffn_conv': nrm(ks[24], (DEPTH, FFN_CONV, 2 * D_FF), FFN_CONV ** -0.5),
        'b_ffn_conv': nrm(ks[25], (DEPTH, 2 * D_FF), 0.02),
        'w_down': nrm(ks[26], (DEPTH, D_FF, D_MODEL), D_FF ** -0.5),
        'w_ple': nrm(ks[27], (DEPTH, PLE_DIM, D_MODEL), PLE_DIM ** -0.5),
        'g_ple': 1.0 + nrm(ks[28], (DEPTH, D_MODEL), 0.05),
        'w_ple_gate': nrm(ks[29], (DEPTH, D_MODEL, D_MODEL), D_MODEL ** -0.5),
        'g_final': 1.0 + nrm(ks[30], (D_MODEL,), 0.05),
    }


def reference(x_prompt, x_sample, p_prompt, p_sample, state_lru_conv, state_lru_h, state_ret, state_ffn_conv,
              g_mix, w_in, w_lru_conv, b_lru_conv, w_r, b_r, w_i, b_i, lru_lambda, w_lru_out,
              gn_g, gn_b, w_ret_out, w_o, g_ffn, w_up, w_ffn_conv, b_ffn_conv, w_down,
              w_ple, g_ple, w_ple_gate, g_final):
    weights = (g_mix, w_in, w_lru_conv, b_lru_conv, w_r, b_r, w_i, b_i, lru_lambda, w_lru_out,
               gn_g, gn_b, w_ret_out, w_o, g_ffn, w_up, w_ffn_conv, b_ffn_conv, w_down,
               w_ple, g_ple, w_ple_gate)
    bp, lp = x_prompt.shape[0], x_prompt.shape[1]
    ls = x_sample.shape[1]
    z_conv_lru = jnp.zeros((DEPTH, bp, LRU_CONV - 1, D_LRU), x_prompt.dtype)
    z_h = jnp.zeros((DEPTH, bp, D_LRU), x_prompt.dtype)
    z_ret = jnp.zeros((DEPTH, bp, RET_HEADS, RET_DK, RET_DV), x_prompt.dtype)
    z_conv_ffn = jnp.zeros((DEPTH, bp, FFN_CONV - 1, 2 * D_FF), x_prompt.dtype)
    pos_prompt = jnp.arange(lp, dtype=jnp.int32)
    y_prompt, cl_p, h_p, r_p, cf_p = _run_group(x_prompt, p_prompt, pos_prompt, z_conv_lru, z_h, z_ret,
                                                z_conv_ffn, weights, g_final)
    pos_sample = PAST_LEN + jnp.arange(ls, dtype=jnp.int32)
    y_sample, cl_s, h_s, r_s, cf_s = _run_group(x_sample, p_sample, pos_sample, state_lru_conv, state_lru_h,
                                                state_ret, state_ffn_conv, weights, g_final)
    return (y_prompt, y_sample, cl_p, h_p, r_p, cf_p, cl_s, h_s, r_s, cf_s)
```

```python
import functools
import math

import jax
import jax.numpy as jnp
from jax import lax
from jax.experimental import pallas as pl
from jax.experimental.pallas import tpu as pltpu

F32 = jnp.float32
BF16 = jnp.bfloat16

D_MODEL = 1024
PLE_DIM = 256
D_LRU = 1024
LRU_BLOCK = 64
LRU_GROUP = 256
N_LRU_GROUPS = D_LRU // LRU_GROUP
LRU_CONV = 4
LRU_C = 8.0
RET_HEADS = 8
RET_DK = 64
RET_DV = 128
N_PAIRS = RET_HEADS // 2
ROPE_BASE = 10000.0
D_FF = 3072
FFN_CONV = 3
FFN_CHUNK = 512
EPS = 1e-6
PAST_LEN = 16384
SUBLANES = 8
LANES = 128

OFF_LX, OFF_LG, OFF_Q, OFF_K, OFF_V, OFF_RG, OFF_GA, OFF_GB = 0, 1024, 2048, 2560, 3072, 4096, 5120, 6144

T_MIX = 256
T_FFN = 256
BS_RET = 32
BS_MIX = 64
VMEM_LIMIT = 56 * 1024 * 1024


def _sigmoid(x):
    return 1.0 / (1.0 + jnp.exp(-x))


def _gelu(x):
    c = math.sqrt(2.0 / math.pi)
    return x * (0.5 * (1.0 + jnp.tanh(c * (x + 0.044715 * (x * x * x)))))


def _rms(x, g):
    ms = jnp.mean(x * x, axis=-1, keepdims=True)
    return x * lax.rsqrt(ms + EPS) * g


def _dot(a, b):
    return jnp.dot(a, b, preferred_element_type=F32)


def _dot_nt(a, b):
    return lax.dot_general(a, b, (((1,), (1,)), ((), ())), preferred_element_type=F32)


def _dot_tn(a, b):
    return lax.dot_general(a, b, (((0,), (0,)), ((), ())), preferred_element_type=F32)


def _rotary_slab(x, cos, sin_signed):
    lane = lax.broadcasted_iota(jnp.int32, x.shape, x.ndim - 1)
    first_half = (lane & (RET_DK - 1)) < (RET_DK // 2)
    other = jnp.where(first_half,
                      pltpu.roll(x, LANES - RET_DK // 2, axis=x.ndim - 1),
                      pltpu.roll(x, RET_DK // 2, axis=x.ndim - 1))
    return x * cos + other * sin_signed


def _lru_gates(xc, wri_ref, br_ref, bi_ref, lam_ref):
    xcb = xc.astype(BF16)
    pre_r, pre_i = [], []
    for g in range(N_LRU_GROUPS):
        pre = _dot(xcb[:, g * LRU_GROUP:(g + 1) * LRU_GROUP], wri_ref[g])
        pre_r.append(pre[:, :LRU_GROUP])
        pre_i.append(pre[:, LRU_GROUP:])
    r = _sigmoid(jnp.concatenate(pre_r, axis=1) + br_ref[...])
    ig = _sigmoid(jnp.concatenate(pre_i, axis=1) + bi_ref[...])
    lam = lam_ref[...]
    softplus_neg_lam = jnp.maximum(-lam, 0.0) + jnp.log1p(jnp.exp(-jnp.abs(lam)))
    log_a = (-LRU_C * r) * softplus_neg_lam
    a = jnp.exp(log_a)
    u = jnp.sqrt(1.0 - a * a) * (ig * xc)
    return a, u


def _scan_within_groups(a3, u3):
    row = lax.broadcasted_iota(jnp.int32, (1,) + a3.shape[1:], 1)
    for s in (1, 2, 4):
        valid = row >= s
        a_prev = jnp.where(valid, pltpu.roll(a3, s, axis=1), 1.0)
        u_prev = jnp.where(valid, pltpu.roll(u3, s, axis=1), 0.0)
        u3 = a3 * u_prev + u3
        a3 = a3 * a_prev
    return a3, u3


def _head_norm_gate(o, rg, gng_ref, gnb_ref):
    parts = []
    for h in range(RET_HEADS):
        oh = o[:, h * RET_DV:(h + 1) * RET_DV]
        mu = jnp.mean(oh, axis=-1, keepdims=True)
        ctr = oh - mu
        var = jnp.mean(ctr * ctr, axis=-1, keepdims=True)
        parts.append(ctr * lax.rsqrt(var + EPS))
    y = jnp.concatenate(parts, axis=1) * gng_ref[...] + gnb_ref[...]
    return y * (rg * _sigmoid(rg))


def _merge_out(x, nxb, ya, yb, win_ref, wo_ref):
    ga = _dot(nxb, win_ref[:, OFF_GA:OFF_GA + D_MODEL])
    gb = _dot(nxb, win_ref[:, OFF_GB:OFF_GB + D_MODEL])
    merged = _sigmoid(ga) * ya + _sigmoid(gb) * yb
    return x + _dot(merged.astype(BF16), wo_ref[...])


def _mixer_prompt_kernel(x_ref, cos_ref, sin_ref, dmask_ref, qdec_ref, kdec_ref, cdec_ref,
                         gmix_ref, win_ref, wconv_ref, bconv_ref, wri_ref, br_ref, bi_ref, lam_ref,
                         wlo_ref, gng_ref, gnb_ref, wro_ref, wo_ref,
                         x1_ref, convst_ref, hst_ref, sst_ref,
                         z_ref, a_ref, b_ref, hs_ref, hc_ref, s_ref):
    T = T_MIX
    t = pl.program_id(1)

    @pl.when(t == 0)
    def _():
        z_ref[0:SUBLANES, :] = jnp.zeros((SUBLANES, D_LRU), F32)
        hc_ref[...] = jnp.zeros_like(hc_ref)
        s_ref[...] = jnp.zeros_like(s_ref)

    x = x_ref[...]
    nxb = _rms(x, gmix_ref[...]).astype(BF16)

    lx = _dot(nxb, win_ref[:, OFF_LX:OFF_LX + D_LRU])
    z_ref[SUBLANES:SUBLANES + T, :] = lx
    xc = bconv_ref[...] + lx * wconv_ref[LRU_CONV - 1:LRU_CONV, :]
    for j in range(LRU_CONV - 1):
        start = SUBLANES - (LRU_CONV - 1) + j
        xc = xc + z_ref[start:start + T, :] * wconv_ref[j:j + 1, :]
    a, u = _lru_gates(xc, wri_ref, br_ref, bi_ref, lam_ref)
    G = T // SUBLANES
    A3, B3 = _scan_within_groups(a.reshape(G, SUBLANES, D_LRU), u.reshape(G, SUBLANES, D_LRU))
    a_ref[...] = A3.reshape(T, D_LRU)
    b_ref[...] = B3.reshape(T, D_LRU)

    def carry_step(g, h_in):
        r0 = pl.multiple_of(g * SUBLANES, SUBLANES)
        h = a_ref[pl.ds(r0, SUBLANES), :] * h_in + b_ref[pl.ds(r0, SUBLANES), :]
        hs_ref[pl.ds(r0, SUBLANES), :] = h
        return jnp.broadcast_to(h[SUBLANES - 1:SUBLANES, :], (SUBLANES, D_LRU))

    h_last = lax.fori_loop(0, G, carry_step, hc_ref[...])
    hc_ref[...] = h_last
    lg = _dot(nxb, win_ref[:, OFF_LG:OFF_LG + D_LRU])
    ya = _dot((hs_ref[...] * _gelu(lg)).astype(BF16), wlo_ref[...])

    q = _dot(nxb, win_ref[:, OFF_Q:OFF_Q + RET_HEADS * RET_DK])
    k = _dot(nxb, win_ref[:, OFF_K:OFF_K + RET_HEADS * RET_DK])
    v = _dot(nxb, win_ref[:, OFF_V:OFF_V + RET_HEADS * RET_DV])
    cos = cos_ref[...]
    sin_signed = sin_ref[...]
    lane = lax.broadcasted_iota(jnp.int32, (T, LANES), 1)
    o_parts = []
    for j in range(N_PAIRS):
        sl = slice(j * LANES, (j + 1) * LANES)
        qr = _rotary_slab(q[:, sl], cos, sin_signed)
        kr = _rotary_slab(k[:, sl], cos, sin_signed) * (RET_DK ** -0.5)
        kb = kr.astype(BF16)
        s_pair = s_ref[j]
        s_pair_b = s_pair.astype(BF16)
        for half in range(2):
            h = 2 * j + half
            qm = jnp.where((lane >> 6) == half, qr, 0.0).astype(BF16)
            p = (_dot_nt(qm, kb) * dmask_ref[h]).astype(BF16)
            vb = v[:, h * RET_DV:(h + 1) * RET_DV].astype(BF16)
            o_parts.append(_dot(p, vb) + _dot(qm, s_pair_b) * qdec_ref[:, h * RET_DV:(h + 1) * RET_DV])
        kd = (kr * kdec_ref[:, sl]).astype(BF16)
        upd = _dot_tn(kd, v[:, 2 * j * RET_DV:(2 * j + 2) * RET_DV].astype(BF16))
        s_ref[j] = s_pair * cdec_ref[j] + jnp.concatenate(
            [upd[0:RET_DK, 0:RET_DV], upd[RET_DK:2 * RET_DK, RET_DV:2 * RET_DV]], axis=0)
    o = jnp.concatenate(o_parts, axis=1)
    rg = _dot(nxb, win_ref[:, OFF_RG:OFF_RG + RET_HEADS * RET_DV])
    yb = _dot(_head_norm_gate(o, rg, gng_ref, gnb_ref).astype(BF16), wro_ref[...])

    x1_ref[...] = _merge_out(x, nxb, ya, yb, win_ref, wo_ref)

    z_ref[0:SUBLANES, :] = z_ref[T:T + SUBLANES, :]

    @pl.when(t == pl.num_programs(1) - 1)
    def _():
        convst_ref[...] = z_ref[SUBLANES - (LRU_CONV - 1):SUBLANES, :]
        hst_ref[...] = h_last[0:1, :]
        sst_ref[...] = s_ref[...]


def _ret_sample_kernel(x_ref, st_ref, cos_ref, sin_ref, d64_ref, qdec_ref, kdec_ref,
                       gmix_ref, wqkv_ref,
                       o_ref, so_ref,
                       q_ref, k_ref, v_ref):
    bs = BS_RET
    nq = RET_HEADS * RET_DK
    nxb = _rms(x_ref[...], gmix_ref[...]).astype(BF16)
    qkv = _dot(nxb, wqkv_ref[...])
    cos = cos_ref[...][None]
    sin_signed = sin_ref[...][None]
    for j in range(N_PAIRS):
        sl = slice(j * LANES, (j + 1) * LANES)
        q3 = qkv[:, j * LANES:(j + 1) * LANES].reshape(bs, SUBLANES, LANES)
        k3 = qkv[:, nq + j * LANES:nq + (j + 1) * LANES].reshape(bs, SUBLANES, LANES)
        q_ref[:, sl] = _rotary_slab(q3, cos, sin_signed).reshape(bs * SUBLANES, LANES)
        k_ref[:, sl] = (_rotary_slab(k3, cos, sin_signed) * (RET_DK ** -0.5)).reshape(bs * SUBLANES, LANES)
    v_ref[...] = qkv[:, 2 * nq:]

    lane = lax.broadcasted_iota(jnp.int32, (SUBLANES, nq), 1)
    gammas = [1.0 - 2.0 ** (-5.0 - h) for h in range(RET_HEADS)]

    def per_seq(s, carry):
        r0 = pl.multiple_of(s * SUBLANES, SUBLANES)
        qs = q_ref[pl.ds(r0, SUBLANES), :]
        ks = k_ref[pl.ds(r0, SUBLANES), :]
        vs = v_ref[pl.ds(r0, SUBLANES), :]
        vb = vs.astype(BF16)
        s0 = st_ref[s]
        qbd = jnp.concatenate([jnp.where((lane >> 6) == h, qs, 0.0) for h in range(RET_HEADS)],
                              axis=0).astype(BF16)
        o_state = _dot(qbd, s0.reshape(RET_HEADS * RET_DK, RET_DV).astype(BF16)) * qdec_ref[...]
        p = (_dot_nt(qbd, ks.astype(BF16)) * d64_ref[...]).astype(BF16)
        o_intra = _dot(p, vb)
        o_ref[pl.ds(r0, SUBLANES), :] = jnp.concatenate(
            [o_intra[h * SUBLANES:(h + 1) * SUBLANES, h * RET_DV:(h + 1) * RET_DV]
             + o_state[h * SUBLANES:(h + 1) * SUBLANES, :] for h in range(RET_HEADS)], axis=1)
        kd = (ks * kdec_ref[...]).astype(BF16)
        for j in range(N_PAIRS):
            upd = _dot_tn(kd[:, j * LANES:(j + 1) * LANES], vb[:, 2 * j * RET_DV:(2 * j + 2) * RET_DV])
            so_ref[s, 2 * j] = s0[2 * j] * (gammas[2 * j] ** SUBLANES) + upd[0:RET_DK, 0:RET_DV]
            so_ref[s, 2 * j + 1] = (s0[2 * j + 1] * (gammas[2 * j + 1] ** SUBLANES)
                                    + upd[RET_DK:2 * RET_DK, RET_DV:2 * RET_DV])
        return carry

    lax.fori_loop(0, bs, per_seq, 0)


def _mixer_sample_kernel(x_ref, o_ref, cst_ref, h0_ref,
                         gmix_ref, win_ref, wconv_ref, bconv_ref, wri_ref, br_ref, bi_ref, lam_ref,
                         wlo_ref, gng_ref, gnb_ref, wro_ref, wo_ref,
                         x1_ref, convst_ref, hst_ref,
                         z_ref):
    bs = BS_MIX
    x = x_ref[...]
    nxb = _rms(x, gmix_ref[...]).astype(BF16)

    lx = _dot(nxb, win_ref[:, OFF_LX:OFF_LX + D_LRU])
    lx3 = lx.reshape(bs, SUBLANES, D_LRU)
    z_ref[:, SUBLANES - (LRU_CONV - 1):SUBLANES, :] = cst_ref[...]
    z_ref[:, SUBLANES:2 * SUBLANES, :] = lx3
    xc3 = bconv_ref[...][None] + lx3 * wconv_ref[LRU_CONV - 1:LRU_CONV, :][None]
    for j in range(LRU_CONV - 1):
        start = SUBLANES - (LRU_CONV - 1) + j
        xc3 = xc3 + z_ref[:, start:start + SUBLANES, :] * wconv_ref[j:j + 1, :][None]
    convst_ref[...] = z_ref[:, 2 * SUBLANES - (LRU_CONV - 1):2 * SUBLANES, :]
    xc = xc3.reshape(bs * SUBLANES, D_LRU)
    a, u = _lru_gates(xc, wri_ref, br_ref, bi_ref, lam_ref)
    A3, B3 = _scan_within_groups(a.reshape(bs, SUBLANES, D_LRU), u.reshape(bs, SUBLANES, D_LRU))
    hs3 = A3 * h0_ref[...] + B3
    hst_ref[...] = hs3[:, SUBLANES - 1:SUBLANES, :]
    lg = _dot(nxb, win_ref[:, OFF_LG:OFF_LG + D_LRU])
    ya = _dot((hs3.reshape(bs * SUBLANES, D_LRU) * _gelu(lg)).astype(BF16), wlo_ref[...])

    rg = _dot(nxb, win_ref[:, OFF_RG:OFF_RG + RET_HEADS * RET_DV])
    yb = _dot(_head_norm_gate(o_ref[...], rg, gng_ref, gnb_ref).astype(BF16), wro_ref[...])

    x1_ref[...] = _merge_out(x, nxb, ya, yb, win_ref, wo_ref)


def _ffn_kernel(x_ref, p_ref, *rest, per_sequence_state):
    if per_sequence_state:
        (cst_ref, gffn_ref, wup_ref, wconv_ref, bconv_ref, wdown_ref, wple_ref, gple_ref, wgate_ref, gfin_ref,
         y_ref, convst_ref, zg_ref, zv_ref) = rest
    else:
        (gffn_ref, wup_ref, wconv_ref, bconv_ref, wdown_ref, wple_ref, gple_ref, wgate_ref, gfin_ref,
         y_ref, convst_ref, zg_ref, zv_ref, carry_ref) = rest
        t = pl.program_id(1)

        @pl.when(t == 0)
        def _():
            carry_ref[...] = jnp.zeros_like(carry_ref)

    x = x_ref[...]
    rows = x.shape[0]
    nxb = _rms(x, gffn_ref[...]).astype(BF16)
    keep = FFN_CONV - 1
    acc = jnp.zeros((rows, D_MODEL), F32)
    for c in range(D_FF // FFN_CHUNK):
        acts = []
        for z_ref, off in ((zg_ref, c * FFN_CHUNK), (zv_ref, D_FF + c * FFN_CHUNK)):
            cols = slice(off, off + FFN_CHUNK)
            up = _dot(nxb, wup_ref[:, cols])
            w = wconv_ref[:, cols]
            if per_sequence_state:
                bs = rows // SUBLANES
                up3 = up.reshape(bs, SUBLANES, FFN_CHUNK)
                z_ref[:, SUBLANES - keep:SUBLANES, :] = cst_ref[:, :, cols]
                z_ref[:, SUBLANES:2 * SUBLANES, :] = up3
                y3 = bconv_ref[:, cols][None] + up3 * w[keep:keep + 1][None]
                for j in range(keep):
                    start = SUBLANES - keep + j
                    y3 = y3 + z_ref[:, start:start + SUBLANES, :] * w[j:j + 1][None]
                convst_ref[:, :, cols] = z_ref[:, 2 * SUBLANES - keep:2 * SUBLANES, :]
                acts.append(y3.reshape(rows, FFN_CHUNK))
            else:
                z_ref[0:SUBLANES, :] = carry_ref[:, cols]
                z_ref[SUBLANES:SUBLANES + rows, :] = up
                y = bconv_ref[:, cols] + up * w[keep:keep + 1]
                for j in range(keep):
                    start = SUBLANES - keep + j
                    y = y + z_ref[start:start + rows, :] * w[j:j + 1]
                carry_ref[:, cols] = z_ref[rows:rows + SUBLANES, :]
                acts.append(y)
        act = (_gelu(acts[0]) * acts[1]).astype(BF16)
        acc = acc + _dot(act, wdown_ref[c * FFN_CHUNK:(c + 1) * FFN_CHUNK, :])
    x2 = x + acc

    e = _rms(_dot(p_ref[...].astype(BF16), wple_ref[...]), gple_ref[...])
    x3 = x2 + _sigmoid(_dot(x2.astype(BF16), wgate_ref[...])) * e
    y_ref[...] = _rms(x3, gfin_ref[...])

    if not per_sequence_state:
        @pl.when(t == pl.num_programs(1) - 1)
        def _():
            convst_ref[...] = carry_ref[SUBLANES - keep:SUBLANES, :]


def _const_spec(shape):
    n = len(shape)
    return pl.BlockSpec(shape, lambda *_: (0,) * n, pipeline_mode=pl.Buffered(1))


def _row(v):
    return v.reshape(1, -1)


def _rotary_tables(pos):
    half = RET_DK // 2
    inv = ROPE_BASE ** (-jnp.arange(half, dtype=F32) / half)
    ang = pos.astype(F32)[:, None] * inv[None, :]
    cos, sin = jnp.cos(ang), jnp.sin(ang)
    return jnp.tile(cos, (1, LANES // half)), jnp.tile(jnp.concatenate([-sin, sin], axis=1), (1, LANES // RET_DK))


def _decay_tables(chunk):
    log_gamma = jnp.log1p(-(2.0 ** (-5.0 - jnp.arange(RET_HEADS, dtype=F32))))
    idx = jnp.arange(chunk)
    rel = idx[:, None] - idx[None, :]
    dmask = jnp.where(rel[None] >= 0,
                      jnp.exp(log_gamma[:, None, None] * jnp.maximum(rel, 0).astype(F32)[None]), 0.0)
    q_decay = jnp.exp(log_gamma[None, :] * (idx + 1).astype(F32)[:, None])
    k_decay = jnp.exp(log_gamma[None, :] * (chunk - 1 - idx).astype(F32)[:, None])
    chunk_decay = jnp.exp(log_gamma * chunk)
    return dmask, q_decay, k_decay, chunk_decay


def _block_diag_gates(w_r, w_i):
    per = LRU_GROUP // LRU_BLOCK
    eye = jnp.eye(per, dtype=w_r.dtype)

    def pack(w):
        w4 = w.reshape(N_LRU_GROUPS, per, LRU_BLOCK, LRU_BLOCK)
        return jnp.einsum('gbij,bc->gbicj', w4, eye).reshape(N_LRU_GROUPS, LRU_GROUP, LRU_GROUP)

    return jnp.concatenate([pack(w_r), pack(w_i)], axis=-1).astype(BF16)


def _params(semantics):
    return pltpu.CompilerParams(dimension_semantics=semantics, vmem_limit_bytes=VMEM_LIMIT)


def kernel(x_prompt, x_sample, p_prompt, p_sample, state_lru_conv, state_lru_h, state_ret, state_ffn_conv, g_mix, w_in, w_lru_conv, b_lru_conv, w_r, b_r, w_i, b_i, lru_lambda, w_lru_out, gn_g, gn_b, w_ret_out, w_o, g_ffn, w_up, w_ffn_conv, b_ffn_conv, w_down, w_ple, g_ple, w_ple_gate, g_final):
    depth = g_mix.shape[0]
    assert depth == 1
    bp, lp, _ = x_prompt.shape
    bsq, ls, _ = x_sample.shape
    assert ls == SUBLANES and lp % T_MIX == 0 and lp % T_FFN == 0
    assert bsq % BS_RET == 0 and bsq % BS_MIX == 0
    hdv = RET_HEADS * RET_DV
    hdk = RET_HEADS * RET_DK

    win_b = w_in[0].astype(BF16)
    wri_b = _block_diag_gates(w_r[0], w_i[0])
    wlo_b = w_lru_out[0].astype(BF16)
    wro_b = w_ret_out[0].astype(BF16)
    wo_b = w_o[0].astype(BF16)
    wup_b = w_up[0].astype(BF16)
    wdown_b = w_down[0].astype(BF16)
    wple_b = w_ple[0].astype(BF16)
    wgate_b = w_ple_gate[0].astype(BF16)
    mixer_weights = (_row(g_mix[0]), win_b, w_lru_conv[0], _row(b_lru_conv[0]), wri_b, _row(b_r[0]), _row(b_i[0]),
                     _row(lru_lambda[0]), wlo_b, _row(gn_g[0]), _row(gn_b[0]), wro_b, wo_b)
    mixer_weight_specs = [_const_spec(w.shape) for w in mixer_weights]
    ffn_weights = (_row(g_ffn[0]), wup_b, w_ffn_conv[0], _row(b_ffn_conv[0]), wdown_b, wple_b, _row(g_ple[0]),
                   wgate_b, _row(g_final))
    ffn_weight_specs = [_const_spec(w.shape) for w in ffn_weights]

    cos_p, sin_p = _rotary_tables(jnp.arange(lp, dtype=jnp.int32))
    dmask, q_decay, k_decay, chunk_decay = _decay_tables(T_MIX)
    qdec_p = jnp.repeat(q_decay, RET_DV, axis=1)
    kdec_p = jnp.repeat(k_decay, RET_DK, axis=1)
    cdec_p = jnp.broadcast_to(jnp.repeat(chunk_decay, RET_DK).reshape(N_PAIRS, 2 * RET_DK, 1),
                              (N_PAIRS, 2 * RET_DK, RET_DV))
    nt = lp // T_MIX
    tile = lambda w: pl.BlockSpec((None, T_MIX, w), lambda b, t: (b, t, 0))
    x1_p, convst_p, hst_p, sst_p = pl.pallas_call(
        _mixer_prompt_kernel,
        grid=(bp, nt),
        in_specs=[tile(D_MODEL),
                  pl.BlockSpec((T_MIX, LANES), lambda b, t: (t, 0)),
                  pl.BlockSpec((T_MIX, LANES), lambda b, t: (t, 0)),
                  _const_spec(dmask.shape), _const_spec(qdec_p.shape), _const_spec(kdec_p.shape),
                  _const_spec(cdec_p.shape)] + mixer_weight_specs,
        out_specs=[tile(D_MODEL),
                   pl.BlockSpec((None, LRU_CONV - 1, D_LRU), lambda b, t: (b, 0, 0)),
                   pl.BlockSpec((None, 1, D_LRU), lambda b, t: (b, 0, 0)),
                   pl.BlockSpec((None, N_PAIRS, 2 * RET_DK, RET_DV), lambda b, t: (b, 0, 0, 0))],
        out_shape=[jax.ShapeDtypeStruct((bp, lp, D_MODEL), F32),
                   jax.ShapeDtypeStruct((bp, LRU_CONV - 1, D_LRU), F32),
                   jax.ShapeDtypeStruct((bp, 1, D_LRU), F32),
                   jax.ShapeDtypeStruct((bp, N_PAIRS, 2 * RET_DK, RET_DV), F32)],
        scratch_shapes=[pltpu.VMEM((T_MIX + 2 * SUBLANES, D_LRU), F32),
                        pltpu.VMEM((T_MIX, D_LRU), F32), pltpu.VMEM((T_MIX, D_LRU), F32),
                        pltpu.VMEM((T_MIX, D_LRU), F32), pltpu.VMEM((SUBLANES, D_LRU), F32),
                        pltpu.VMEM((N_PAIRS, 2 * RET_DK, RET_DV), F32)],
        compiler_params=_params(("arbitrary", "arbitrary")),
        name="mixer_prompt",
    )(x_prompt, cos_p, sin_p, dmask, qdec_p, kdec_p, cdec_p, *mixer_weights)

    ntf = lp // T_FFN
    tilef = lambda w: pl.BlockSpec((None, T_FFN, w), lambda b, t: (b, t, 0))
    y_p, ffnst_p = pl.pallas_call(
        functools.partial(_ffn_kernel, per_sequence_state=False),
        grid=(bp, ntf),
        in_specs=[tilef(D_MODEL), tilef(PLE_DIM)] + ffn_weight_specs,
        out_specs=[tilef(D_MODEL), pl.BlockSpec((None, FFN_CONV - 1, 2 * D_FF), lambda b, t: (b, 0, 0))],
        out_shape=[jax.ShapeDtypeStruct((bp, lp, D_MODEL), F32),
                   jax.ShapeDtypeStruct((bp, FFN_CONV - 1, 2 * D_FF), F32)],
        scratch_shapes=[pltpu.VMEM((T_FFN + 2 * SUBLANES, FFN_CHUNK), F32),
                        pltpu.VMEM((T_FFN + 2 * SUBLANES, FFN_CHUNK), F32),
                        pltpu.VMEM((SUBLANES, 2 * D_FF), F32)],
        compiler_params=_params(("arbitrary", "arbitrary")),
        name="ffn_prompt",
    )(x1_p, p_prompt[0], *ffn_weights)

    rows_s = bsq * ls
    xs2 = x_sample.reshape(rows_s, D_MODEL)
    cos_s, sin_s = _rotary_tables(PAST_LEN + jnp.arange(ls, dtype=jnp.int32))
    dmask8, q_decay8, k_decay8, _ = _decay_tables(ls)
    d64 = dmask8.reshape(RET_HEADS * ls, ls)
    qdec64 = jnp.broadcast_to(q_decay8.T.reshape(RET_HEADS * ls, 1), (RET_HEADS * ls, RET_DV))
    kdec8 = jnp.repeat(k_decay8, RET_DK, axis=1)
    wqkv_b = win_b[:, OFF_Q:OFF_RG]
    rr = BS_RET * ls
    o_s, sst_s = pl.pallas_call(
        _ret_sample_kernel,
        grid=(bsq // BS_RET,),
        in_specs=[pl.BlockSpec((rr, D_MODEL), lambda i: (i, 0)),
                  pl.BlockSpec((BS_RET, RET_HEADS, RET_DK, RET_DV), lambda i: (i, 0, 0, 0)),
                  _const_spec(cos_s.shape), _const_spec(sin_s.shape), _const_spec(d64.shape),
                  _const_spec(qdec64.shape), _const_spec(kdec8.shape),
                  _const_spec((1, D_MODEL)), _const_spec(wqkv_b.shape)],
        out_specs=[pl.BlockSpec((rr, hdv), lambda i: (i, 0)),
                   pl.BlockSpec((BS_RET, RET_HEADS, RET_DK, RET_DV), lambda i: (i, 0, 0, 0))],
        out_shape=[jax.ShapeDtypeStruct((rows_s, hdv), F32),
                   jax.ShapeDtypeStruct((bsq, RET_HEADS, RET_DK, RET_DV), F32)],
        scratch_shapes=[pltpu.VMEM((rr, hdk), F32), pltpu.VMEM((rr, hdk), F32), pltpu.VMEM((rr, hdv), F32)],
        compiler_params=_params(("arbitrary",)),
        name="ret_sample",
    )(xs2, state_ret[0], cos_s, sin_s, d64, qdec64, kdec8, _row(g_mix[0]), wqkv_b)

    rm = BS_MIX * ls
    x1_s, convst_s, hst_s = pl.pallas_call(
        _mixer_sample_kernel,
        grid=(bsq // BS_MIX,),
        in_specs=[pl.BlockSpec((rm, D_MODEL), lambda i: (i, 0)),
                  pl.BlockSpec((rm, hdv), lambda i: (i, 0)),
                  pl.BlockSpec((BS_MIX, LRU_CONV - 1, D_LRU), lambda i: (i, 0, 0)),
                  pl.BlockSpec((BS_MIX, 1, D_LRU), lambda i: (i, 0, 0))] + mixer_weight_specs,
        out_specs=[pl.BlockSpec((rm, D_MODEL), lambda i: (i, 0)),
                   pl.BlockSpec((BS_MIX, LRU_CONV - 1, D_LRU), lambda i: (i, 0, 0)),
                   pl.BlockSpec((BS_MIX, 1, D_LRU), lambda i: (i, 0, 0))],
        out_shape=[jax.ShapeDtypeStruct((rows_s, D_MODEL), F32),
                   jax.ShapeDtypeStruct((bsq, LRU_CONV - 1, D_LRU), F32),
                   jax.ShapeDtypeStruct((bsq, 1, D_LRU), F32)],
        scratch_shapes=[pltpu.VMEM((BS_MIX, 2 * SUBLANES, D_LRU), F32)],
        compiler_params=_params(("arbitrary",)),
        name="mixer_sample",
    )(xs2, o_s, state_lru_conv[0], state_lru_h[0].reshape(bsq, 1, D_LRU), *mixer_weights)

    y_s, ffnst_s = pl.pallas_call(
        functools.partial(_ffn_kernel, per_sequence_state=True),
        grid=(bsq // BS_MIX,),
        in_specs=[pl.BlockSpec((rm, D_MODEL), lambda i: (i, 0)),
                  pl.BlockSpec((rm, PLE_DIM), lambda i: (i, 0)),
                  pl.BlockSpec((BS_MIX, FFN_CONV - 1, 2 * D_FF), lambda i: (i, 0, 0))] + ffn_weight_specs,
        out_specs=[pl.BlockSpec((rm, D_MODEL), lambda i: (i, 0)),
                   pl.BlockSpec((BS_MIX, FFN_CONV - 1, 2 * D_FF), lambda i: (i, 0, 0))],
        out_shape=[jax.ShapeDtypeStruct((rows_s, D_MODEL), F32),
                   jax.ShapeDtypeStruct((bsq, FFN_CONV - 1, 2 * D_FF), F32)],
        scratch_shapes=[pltpu.VMEM((BS_MIX, 2 * SUBLANES, FFN_CHUNK), F32),
                        pltpu.VMEM((BS_MIX, 2 * SUBLANES, FFN_CHUNK), F32)],
        compiler_params=_params(("arbitrary",)),
        name="ffn_sample",
    )(x1_s, p_sample[0].reshape(rows_s, PLE_DIM), state_ffn_conv[0], *ffn_weights)

    return (y_p,
            y_s.reshape(bsq, ls, D_MODEL),
            convst_p[None],
            hst_p.reshape(1, bp, D_LRU),
            sst_p.reshape(1, bp, RET_HEADS, RET_DK, RET_DV),
            ffnst_p[None],
            convst_s[None],
            hst_s.reshape(1, bsq, D_LRU),
            sst_s[None],
            ffnst_s[None])
```

```python
import functools
import math

import jax
import jax.numpy as jnp
from jax import lax
from jax.experimental import pallas as pl
from jax.experimental.pallas import tpu as pltpu

F32 = jnp.float32
BF16 = jnp.bfloat16

D_MODEL = 1024
PLE_DIM = 256
D_LRU = 1024
LRU_BLOCK = 64
LRU_GROUP = 256
N_LRU_GROUPS = D_LRU // LRU_GROUP
LRU_CONV = 4
LRU_C = 8.0
RET_HEADS = 8
RET_DK = 64
RET_DV = 128
N_PAIRS = RET_HEADS // 2
ROPE_BASE = 10000.0
D_FF = 3072
FFN_CONV = 3
FFN_CHUNK = 512
EPS = 1e-6
PAST_LEN = 16384
SUBLANES = 8
LANES = 128

W_BLOCK = 512
BLK_LX, BLK_LG, BLK_Q, BLK_K, BLK_V, BLK_RG, BLK_GA, BLK_GB = (0, 2), (2, 2), (4, 1), (5, 1), (6, 2), (8, 2), (10, 2), (12, 2)

T_MIX = 256
T_FFN = 512
BS_RET = 32
BS_MIX = 64
VMEM_LIMIT = 56 * 1024 * 1024


def _sigmoid(x):
    return 1.0 / (1.0 + jnp.exp(-x))


def _gelu(x):
    c = math.sqrt(2.0 / math.pi)
    return x * (0.5 + 0.5 * jnp.tanh(x * (c + (c * 0.044715) * (x * x))))


def _rms(x, g):
    ms = jnp.mean(x * x, axis=-1, keepdims=True)
    return x * lax.rsqrt(ms + EPS) * g


def _dot(a, b):
    return jnp.dot(a, b, preferred_element_type=F32)


def _dot_blocks(a, w_ref, blocks=None):
    first, count = blocks if blocks is not None else (0, w_ref.shape[0])
    parts = [_dot(a, w_ref[first + i]) for i in range(count)]
    return parts[0] if count == 1 else jnp.concatenate(parts, axis=1)


def _dot_nt(a, b):
    return lax.dot_general(a, b, (((1,), (1,)), ((), ())), preferred_element_type=F32)


def _dot_tn(a, b):
    return lax.dot_general(a, b, (((0,), (0,)), ((), ())), preferred_element_type=F32)


def _rotary_slab(x, cos, sin_signed):
    lane = lax.broadcasted_iota(jnp.int32, x.shape, x.ndim - 1)
    first_half = (lane & (RET_DK - 1)) < (RET_DK // 2)
    other = jnp.where(first_half,
                      pltpu.roll(x, LANES - RET_DK // 2, axis=x.ndim - 1),
                      pltpu.roll(x, RET_DK // 2, axis=x.ndim - 1))
    return x * cos + other * sin_signed


def _rows_from_lanes(ref, steps, width):
    return jnp.concatenate([ref[:, t * width:(t + 1) * width] for t in range(steps)], axis=0)


def _lru_gates(xc, wri_ref, br_ref, bi_ref, lam_ref):
    xcb = xc.astype(BF16)
    pre_r, pre_i = [], []
    for g in range(N_LRU_GROUPS):
        pre = _dot(xcb[:, g * LRU_GROUP:(g + 1) * LRU_GROUP], wri_ref[g])
        pre_r.append(pre[:, :LRU_GROUP])
        pre_i.append(pre[:, LRU_GROUP:])
    r = _sigmoid(jnp.concatenate(pre_r, axis=1) + br_ref[...])
    ig = _sigmoid(jnp.concatenate(pre_i, axis=1) + bi_ref[...])
    lam = lam_ref[...]
    softplus_neg_lam = jnp.maximum(-lam, 0.0) + jnp.log1p(jnp.exp(-jnp.abs(lam)))
    log_a = (-LRU_C * r) * softplus_neg_lam
    a = jnp.exp(log_a)
    u = jnp.sqrt(1.0 - a * a) * (ig * xc)
    return a, u


def _scan_within_groups(a3, u3):
    row = lax.broadcasted_iota(jnp.int32, (1,) + a3.shape[1:], 1)
    for s in (1, 2, 4):
        valid = row >= s
        a_prev = jnp.where(valid, pltpu.roll(a3, s, axis=1), 1.0)
        u_prev = jnp.where(valid, pltpu.roll(u3, s, axis=1), 0.0)
        u3 = a3 * u_prev + u3
        a3 = a3 * a_prev
    return a3, u3


def _head_norm_gate(o, rg, gng_ref, gnb_ref):
    parts = []
    for h in range(RET_HEADS):
        oh = o[:, h * RET_DV:(h + 1) * RET_DV]
        mu = jnp.mean(oh, axis=-1, keepdims=True)
        ctr = oh - mu
        var = jnp.mean(ctr * ctr, axis=-1, keepdims=True)
        parts.append(ctr * lax.rsqrt(var + EPS))
    y = jnp.concatenate(parts, axis=1) * gng_ref[...] + gnb_ref[...]
    return y * (rg * _sigmoid(rg))


def _merge_out(x, nxb, ya, yb, win_ref, wo_ref):
    ga = _dot_blocks(nxb, win_ref, BLK_GA)
    gb = _dot_blocks(nxb, win_ref, BLK_GB)
    merged = _sigmoid(ga) * ya + _sigmoid(gb) * yb
    return x + _dot_blocks(merged.astype(BF16), wo_ref)


def _mixer_prompt_kernel(x_ref, cos_ref, sin_ref, dmask_ref, qdec_ref, kdec_ref, cdec_ref,
                         gmix_ref, win_ref, wconv_ref, bconv_ref, wri_ref, br_ref, bi_ref, lam_ref,
                         wlo_ref, gng_ref, gnb_ref, wro_ref, wo_ref,
                         x1_ref, convst_ref, hst_ref, sst_ref,
                         z_ref, a_ref, b_ref, hs_ref, hc_ref, s_ref):
    T = T_MIX
    t = pl.program_id(1)

    @pl.when(t == 0)
    def _():
        z_ref[0:SUBLANES, :] = jnp.zeros((SUBLANES, D_LRU), F32)
        hc_ref[...] = jnp.zeros_like(hc_ref)
        s_ref[...] = jnp.zeros_like(s_ref)

    x = x_ref[...]
    nxb = _rms(x, gmix_ref[...]).astype(BF16)

    lx = _dot_blocks(nxb, win_ref, BLK_LX)
    z_ref[SUBLANES:SUBLANES + T, :] = lx
    xc = bconv_ref[...] + lx * wconv_ref[LRU_CONV - 1:LRU_CONV, :]
    for j in range(LRU_CONV - 1):
        start = SUBLANES - (LRU_CONV - 1) + j
        xc = xc + z_ref[start:start + T, :] * wconv_ref[j:j + 1, :]
    a, u = _lru_gates(xc, wri_ref, br_ref, bi_ref, lam_ref)
    G = T // SUBLANES
    A3, B3 = _scan_within_groups(a.reshape(G, SUBLANES, D_LRU), u.reshape(G, SUBLANES, D_LRU))
    a_ref[...] = A3.reshape(T, D_LRU)
    b_ref[...] = B3.reshape(T, D_LRU)

    def carry_step(g, h_in):
        r0 = pl.multiple_of(g * SUBLANES, SUBLANES)
        h = a_ref[pl.ds(r0, SUBLANES), :] * h_in + b_ref[pl.ds(r0, SUBLANES), :]
        hs_ref[pl.ds(r0, SUBLANES), :] = h
        return jnp.broadcast_to(h[SUBLANES - 1:SUBLANES, :], (SUBLANES, D_LRU))

    h_last = lax.fori_loop(0, G, carry_step, hc_ref[...])
    hc_ref[...] = h_last
    lg = _dot_blocks(nxb, win_ref, BLK_LG)
    ya = _dot_blocks((hs_ref[...] * _gelu(lg)).astype(BF16), wlo_ref)

    q = _dot_blocks(nxb, win_ref, BLK_Q)
    k = _dot_blocks(nxb, win_ref, BLK_K)
    v = _dot_blocks(nxb, win_ref, BLK_V)
    cos = cos_ref[...]
    sin_signed = sin_ref[...]
    lane = lax.broadcasted_iota(jnp.int32, (T, LANES), 1)
    o_parts = []
    for j in range(N_PAIRS):
        sl = slice(j * LANES, (j + 1) * LANES)
        qr = _rotary_slab(q[:, sl], cos, sin_signed)
        kr = _rotary_slab(k[:, sl], cos, sin_signed) * (RET_DK ** -0.5)
        kb = kr.astype(BF16)
        s_pair = s_ref[j]
        s_pair_b = s_pair.astype(BF16)
        for half in range(2):
            h = 2 * j + half
            qm = jnp.where((lane >> 6) == half, qr, 0.0).astype(BF16)
            p = (_dot_nt(qm, kb) * dmask_ref[h]).astype(BF16)
            vb = v[:, h * RET_DV:(h + 1) * RET_DV].astype(BF16)
            o_parts.append(_dot(p, vb) + _dot(qm, s_pair_b) * qdec_ref[:, h * RET_DV:(h + 1) * RET_DV])
        kd = (kr * kdec_ref[:, sl]).astype(BF16)
        upd = _dot_tn(kd, v[:, 2 * j * RET_DV:(2 * j + 2) * RET_DV].astype(BF16))
        s_ref[j] = s_pair * cdec_ref[j] + jnp.concatenate(
            [upd[0:RET_DK, 0:RET_DV], upd[RET_DK:2 * RET_DK, RET_DV:2 * RET_DV]], axis=0)
    o = jnp.concatenate(o_parts, axis=1)
    rg = _dot_blocks(nxb, win_ref, BLK_RG)
    yb = _dot_blocks(_head_norm_gate(o, rg, gng_ref, gnb_ref).astype(BF16), wro_ref)

    x1_ref[...] = _merge_out(x, nxb, ya, yb, win_ref, wo_ref)

    z_ref[0:SUBLANES, :] = z_ref[T:T + SUBLANES, :]

    @pl.when(t == pl.num_programs(1) - 1)
    def _():
        convst_ref[...] = z_ref[SUBLANES - (LRU_CONV - 1):SUBLANES, :]
        hst_ref[...] = h_last[0:1, :]
        sst_ref[...] = s_ref[...]


def _ret_sample_kernel(x_ref, st_ref, cos_ref, sin_ref, d64_ref, qdec_ref, kdec_ref,
                       gmix_ref, wqkv_ref,
                       o_ref, so_ref,
                       q_ref, k_ref, v_ref):
    bs = BS_RET
    nq = RET_HEADS * RET_DK
    nxb = _rms(x_ref[...], gmix_ref[...]).astype(BF16)
    qkv = _dot_blocks(nxb, wqkv_ref)
    cos = cos_ref[...][None]
    sin_signed = sin_ref[...][None]
    for j in range(N_PAIRS):
        sl = slice(j * LANES, (j + 1) * LANES)
        q3 = qkv[:, j * LANES:(j + 1) * LANES].reshape(bs, SUBLANES, LANES)
        k3 = qkv[:, nq + j * LANES:nq + (j + 1) * LANES].reshape(bs, SUBLANES, LANES)
        q_ref[:, sl] = _rotary_slab(q3, cos, sin_signed).reshape(bs * SUBLANES, LANES)
        k_ref[:, sl] = (_rotary_slab(k3, cos, sin_signed) * (RET_DK ** -0.5)).reshape(bs * SUBLANES, LANES)
    v_ref[...] = qkv[:, 2 * nq:]

    lane = lax.broadcasted_iota(jnp.int32, (SUBLANES, nq), 1)
    gammas = [1.0 - 2.0 ** (-5.0 - h) for h in range(RET_HEADS)]

    def per_seq(s, carry):
        r0 = pl.multiple_of(s * SUBLANES, SUBLANES)
        qs = q_ref[pl.ds(r0, SUBLANES), :]
        ks = k_ref[pl.ds(r0, SUBLANES), :]
        vs = v_ref[pl.ds(r0, SUBLANES), :]
        vb = vs.astype(BF16)
        s0 = st_ref[s]
        qbd = jnp.concatenate([jnp.where((lane >> 6) == h, qs, 0.0) for h in range(RET_HEADS)],
                              axis=0).astype(BF16)
        o_state = _dot(qbd, s0.reshape(RET_HEADS * RET_DK, RET_DV).astype(BF16)) * qdec_ref[...]
        p = (_dot_nt(qbd, ks.astype(BF16)) * d64_ref[...]).astype(BF16)
        o_intra = _dot(p, vb)
        o_ref[pl.ds(r0, SUBLANES), :] = jnp.concatenate(
            [o_intra[h * SUBLANES:(h + 1) * SUBLANES, h * RET_DV:(h + 1) * RET_DV]
             + o_state[h * SUBLANES:(h + 1) * SUBLANES, :] for h in range(RET_HEADS)], axis=1)
        kd = (ks * kdec_ref[...]).astype(BF16)
        for j in range(N_PAIRS):
            upd = _dot_tn(kd[:, j * LANES:(j + 1) * LANES], vb[:, 2 * j * RET_DV:(2 * j + 2) * RET_DV])
            so_ref[s, 2 * j] = s0[2 * j] * (gammas[2 * j] ** SUBLANES) + upd[0:RET_DK, 0:RET_DV]
            so_ref[s, 2 * j + 1] = (s0[2 * j + 1] * (gammas[2 * j + 1] ** SUBLANES)
                                    + upd[RET_DK:2 * RET_DK, RET_DV:2 * RET_DV])
        return carry

    lax.fori_loop(0, bs, per_seq, 0)


def _mixer_sample_kernel(x_ref, o_ref, cst_ref, h0_ref,
                         gmix_ref, win_ref, wconv_ref, bconv_ref, wri_ref, br_ref, bi_ref, lam_ref,
                         wlo_ref, gng_ref, gnb_ref, wro_ref, wo_ref,
                         x1_ref, convst_ref, hst_ref):
    bs = BS_MIX
    steps = SUBLANES
    x = _rows_from_lanes(x_ref, steps, D_MODEL)
    nxb = _rms(x, gmix_ref[...]).astype(BF16)

    lx = _dot_blocks(nxb, win_ref, BLK_LX)
    keep = LRU_CONV - 1
    ext = ([cst_ref[:, j * D_LRU:(j + 1) * D_LRU] for j in range(keep)]
           + [lx[t * bs:(t + 1) * bs] for t in range(steps)])
    xc = jnp.concatenate(
        [bconv_ref[...] + sum(ext[t + j] * wconv_ref[j:j + 1, :] for j in range(LRU_CONV)) for t in range(steps)],
        axis=0)
    for j in range(keep):
        convst_ref[:, j * D_LRU:(j + 1) * D_LRU] = ext[steps + j]
    a, u = _lru_gates(xc, wri_ref, br_ref, bi_ref, lam_ref)
    h = h0_ref[...]
    hs = []
    for t in range(steps):
        h = a[t * bs:(t + 1) * bs] * h + u[t * bs:(t + 1) * bs]
        hs.append(h)
    hst_ref[...] = h
    lg = _dot_blocks(nxb, win_ref, BLK_LG)
    ya = _dot_blocks((jnp.concatenate(hs, axis=0) * _gelu(lg)).astype(BF16), wlo_ref)

    rg = _dot_blocks(nxb, win_ref, BLK_RG)
    o = _rows_from_lanes(o_ref, steps, RET_HEADS * RET_DV)
    yb = _dot_blocks(_head_norm_gate(o, rg, gng_ref, gnb_ref).astype(BF16), wro_ref)

    x1_ref[...] = _merge_out(x, nxb, ya, yb, win_ref, wo_ref).reshape(steps, bs, D_MODEL)


def _ffn_kernel(x_ref, p_ref, *rest, decode):
    keep = FFN_CONV - 1
    if decode:
        (cst_ref, gffn_ref, wup_ref, wconv_ref, bconv_ref, wdown_ref, wple_ref, gple_ref, wgate_ref, gfin_ref,
         y_ref, convst_ref) = rest
        steps, bs = x_ref.shape[0], x_ref.shape[1]
        x = x_ref[...].reshape(steps * bs, D_MODEL)
        p = _rows_from_lanes(p_ref, steps, PLE_DIM)
    else:
        (gffn_ref, wup_ref, wconv_ref, bconv_ref, wdown_ref, wple_ref, gple_ref, wgate_ref, gfin_ref,
         y_ref, convst_ref, z_ref, carry_ref) = rest
        t = pl.program_id(1)

        @pl.when(t == 0)
        def _():
            carry_ref[...] = jnp.zeros_like(carry_ref)

        x = x_ref[...]
        p = p_ref[...]
    rows = x.shape[0]
    nxb = _rms(x, gffn_ref[...]).astype(BF16)
    n_chunks = D_FF // FFN_CHUNK
    act_chunks = []
    for c in range(n_chunks):
        halves = []
        for half in range(2):
            blk = half * n_chunks + c
            cols = slice(blk * FFN_CHUNK, (blk + 1) * FFN_CHUNK)
            up = _dot(nxb, wup_ref[blk])
            w = wconv_ref[:, cols]
            bias = bconv_ref[:, cols]
            if decode:
                ext = ([cst_ref[:, j * 2 * D_FF + cols.start:j * 2 * D_FF + cols.stop] for j in range(keep)]
                       + [up[s * bs:(s + 1) * bs] for s in range(steps)])
                halves.append(jnp.concatenate(
                    [bias + sum(ext[s + j] * w[j:j + 1] for j in range(FFN_CONV)) for s in range(steps)], axis=0))
                for j in range(keep):
                    convst_ref[:, j * 2 * D_FF + cols.start:j * 2 * D_FF + cols.stop] = ext[steps + j]
            else:
                zc_ref = z_ref.at[2 * (c % 2) + half]
                zc_ref[0:SUBLANES, :] = carry_ref[:, cols]
                zc_ref[SUBLANES:SUBLANES + rows, :] = up
                y = bias + up * w[keep:keep + 1]
                for j in range(keep):
                    start = SUBLANES - keep + j
                    y = y + zc_ref[start:start + rows, :] * w[j:j + 1]
                carry_ref[:, cols] = zc_ref[rows:rows + SUBLANES, :]
                halves.append(y)
        act_chunks.append((_gelu(halves[0]) * halves[1]).astype(BF16))
    x2 = x + _dot_blocks(jnp.concatenate(act_chunks, axis=1), wdown_ref)

    e = _rms(_dot_blocks(p.astype(BF16), wple_ref), gple_ref[...])
    x3 = x2 + _sigmoid(_dot_blocks(x2.astype(BF16), wgate_ref)) * e
    y = _rms(x3, gfin_ref[...])

    if decode:
        for s in range(steps):
            y_ref[:, s * D_MODEL:(s + 1) * D_MODEL] = y[s * bs:(s + 1) * bs]
    else:
        y_ref[...] = y

        @pl.when(t == pl.num_programs(1) - 1)
        def _():
            convst_ref[...] = carry_ref[SUBLANES - keep:SUBLANES, :]


def _const_spec(shape):
    n = len(shape)
    return pl.BlockSpec(shape, lambda *_: (0,) * n, pipeline_mode=pl.Buffered(1))


def _row(v):
    return v.reshape(1, -1)


def _rotary_tables(pos):
    half = RET_DK // 2
    inv = ROPE_BASE ** (-jnp.arange(half, dtype=F32) / half)
    ang = pos.astype(F32)[:, None] * inv[None, :]
    cos, sin = jnp.cos(ang), jnp.sin(ang)
    return jnp.tile(cos, (1, LANES // half)), jnp.tile(jnp.concatenate([-sin, sin], axis=1), (1, LANES // RET_DK))


def _decay_tables(chunk):
    log_gamma = jnp.log1p(-(2.0 ** (-5.0 - jnp.arange(RET_HEADS, dtype=F32))))
    idx = jnp.arange(chunk)
    rel = idx[:, None] - idx[None, :]
    dmask = jnp.where(rel[None] >= 0,
                      jnp.exp(log_gamma[:, None, None] * jnp.maximum(rel, 0).astype(F32)[None]), 0.0)
    q_decay = jnp.exp(log_gamma[None, :] * (idx + 1).astype(F32)[:, None])
    k_decay = jnp.exp(log_gamma[None, :] * (chunk - 1 - idx).astype(F32)[:, None])
    chunk_decay = jnp.exp(log_gamma * chunk)
    return dmask, q_decay, k_decay, chunk_decay


def _block_diag_gates(w_r, w_i):
    per = LRU_GROUP // LRU_BLOCK
    eye = jnp.eye(per, dtype=w_r.dtype)

    def pack(w):
        w4 = w.reshape(N_LRU_GROUPS, per, LRU_BLOCK, LRU_BLOCK)
        return jnp.einsum('gbij,bc->gbicj', w4, eye).reshape(N_LRU_GROUPS, LRU_GROUP, LRU_GROUP)

    return jnp.concatenate([pack(w_r), pack(w_i)], axis=-1).astype(BF16)


def _col_blocks(w):
    k, n = w.shape
    return jnp.transpose(w.astype(BF16).reshape(k, n // W_BLOCK, W_BLOCK), (1, 0, 2))


def _params(semantics):
    return pltpu.CompilerParams(dimension_semantics=semantics, vmem_limit_bytes=VMEM_LIMIT)


def kernel(x_prompt, x_sample, p_prompt, p_sample, state_lru_conv, state_lru_h, state_ret, state_ffn_conv, g_mix, w_in, w_lru_conv, b_lru_conv, w_r, b_r, w_i, b_i, lru_lambda, w_lru_out, gn_g, gn_b, w_ret_out, w_o, g_ffn, w_up, w_ffn_conv, b_ffn_conv, w_down, w_ple, g_ple, w_ple_gate, g_final):
    depth = g_mix.shape[0]
    assert depth == 1
    bp, lp, _ = x_prompt.shape
    bsq, ls, _ = x_sample.shape
    assert ls == SUBLANES and lp % T_MIX == 0 and lp % T_FFN == 0
    assert bsq % BS_RET == 0 and bsq % BS_MIX == 0
    hdv = RET_HEADS * RET_DV
    hdk = RET_HEADS * RET_DK

    win_b = _col_blocks(w_in[0])
    wri_b = _block_diag_gates(w_r[0], w_i[0])
    wlo_b = _col_blocks(w_lru_out[0])
    wro_b = _col_blocks(w_ret_out[0])
    wo_b = _col_blocks(w_o[0])
    wup_b = _col_blocks(w_up[0])
    wdown_b = _col_blocks(w_down[0])
    wple_b = _col_blocks(w_ple[0])
    wgate_b = _col_blocks(w_ple_gate[0])
    mixer_weights = (_row(g_mix[0]), win_b, w_lru_conv[0], _row(b_lru_conv[0]), wri_b, _row(b_r[0]), _row(b_i[0]),
                     _row(lru_lambda[0]), wlo_b, _row(gn_g[0]), _row(gn_b[0]), wro_b, wo_b)
    mixer_weight_specs = [_const_spec(w.shape) for w in mixer_weights]
    ffn_weights = (_row(g_ffn[0]), wup_b, w_ffn_conv[0], _row(b_ffn_conv[0]), wdown_b, wple_b, _row(g_ple[0]),
                   wgate_b, _row(g_final))
    ffn_weight_specs = [_const_spec(w.shape) for w in ffn_weights]

    cos_p, sin_p = _rotary_tables(jnp.arange(lp, dtype=jnp.int32))
    dmask, q_decay, k_decay, chunk_decay = _decay_tables(T_MIX)
    qdec_p = jnp.repeat(q_decay, RET_DV, axis=1)
    kdec_p = jnp.repeat(k_decay, RET_DK, axis=1)
    cdec_p = jnp.broadcast_to(jnp.repeat(chunk_decay, RET_DK).reshape(N_PAIRS, 2 * RET_DK, 1),
                              (N_PAIRS, 2 * RET_DK, RET_DV))
    nt = lp // T_MIX
    tile = lambda w: pl.BlockSpec((None, T_MIX, w), lambda b, t: (b, t, 0))
    x1_p, convst_p, hst_p, sst_p = pl.pallas_call(
        _mixer_prompt_kernel,
        grid=(bp, nt),
        in_specs=[tile(D_MODEL),
                  pl.BlockSpec((T_MIX, LANES), lambda b, t: (t, 0)),
                  pl.BlockSpec((T_MIX, LANES), lambda b, t: (t, 0)),
                  _const_spec(dmask.shape), _const_spec(qdec_p.shape), _const_spec(kdec_p.shape),
                  _const_spec(cdec_p.shape)] + mixer_weight_specs,
        out_specs=[tile(D_MODEL),
                   pl.BlockSpec((None, LRU_CONV - 1, D_LRU), lambda b, t: (b, 0, 0)),
                   pl.BlockSpec((None, 1, D_LRU), lambda b, t: (b, 0, 0)),
                   pl.BlockSpec((None, N_PAIRS, 2 * RET_DK, RET_DV), lambda b, t: (b, 0, 0, 0))],
        out_shape=[jax.ShapeDtypeStruct((bp, lp, D_MODEL), F32),
                   jax.ShapeDtypeStruct((bp, LRU_CONV - 1, D_LRU), F32),
                   jax.ShapeDtypeStruct((bp, 1, D_LRU), F32),
                   jax.ShapeDtypeStruct((bp, N_PAIRS, 2 * RET_DK, RET_DV), F32)],
        scratch_shapes=[pltpu.VMEM((T_MIX + 2 * SUBLANES, D_LRU), F32),
                        pltpu.VMEM((T_MIX, D_LRU), F32), pltpu.VMEM((T_MIX, D_LRU), F32),
                        pltpu.VMEM((T_MIX, D_LRU), F32), pltpu.VMEM((SUBLANES, D_LRU), F32),
                        pltpu.VMEM((N_PAIRS, 2 * RET_DK, RET_DV), F32)],
        compiler_params=_params(("arbitrary", "arbitrary")),
        name="mixer_prompt",
    )(x_prompt, cos_p, sin_p, dmask, qdec_p, kdec_p, cdec_p, *mixer_weights)

    ntf = lp // T_FFN
    tilef = lambda w: pl.BlockSpec((None, T_FFN, w), lambda b, t: (b, t, 0))
    y_p, ffnst_p = pl.pallas_call(
        functools.partial(_ffn_kernel, decode=False),
        grid=(bp, ntf),
        in_specs=[tilef(D_MODEL), tilef(PLE_DIM)] + ffn_weight_specs,
        out_specs=[tilef(D_MODEL), pl.BlockSpec((None, FFN_CONV - 1, 2 * D_FF), lambda b, t: (b, 0, 0))],
        out_shape=[jax.ShapeDtypeStruct((bp, lp, D_MODEL), F32),
                   jax.ShapeDtypeStruct((bp, FFN_CONV - 1, 2 * D_FF), F32)],
        scratch_shapes=[pltpu.VMEM((4, T_FFN + 2 * SUBLANES, FFN_CHUNK), F32),
                        pltpu.VMEM((SUBLANES, 2 * D_FF), F32)],
        compiler_params=_params(("arbitrary", "arbitrary")),
        name="ffn_prompt",
    )(x1_p, p_prompt[0], *ffn_weights)

    rows_s = bsq * ls
    xs2 = x_sample.reshape(rows_s, D_MODEL)
    cos_s, sin_s = _rotary_tables(PAST_LEN + jnp.arange(ls, dtype=jnp.int32))
    dmask8, q_decay8, k_decay8, _ = _decay_tables(ls)
    d64 = dmask8.reshape(RET_HEADS * ls, ls)
    qdec64 = jnp.broadcast_to(q_decay8.T.reshape(RET_HEADS * ls, 1), (RET_HEADS * ls, RET_DV))
    kdec8 = jnp.repeat(k_decay8, RET_DK, axis=1)
    rr = BS_RET * ls
    n_qkv = BLK_RG[0] - BLK_Q[0]
    assert BLK_Q[0] % n_qkv == 0
    o_s, sst_s = pl.pallas_call(
        _ret_sample_kernel,
        grid=(bsq // BS_RET,),
        in_specs=[pl.BlockSpec((rr, D_MODEL), lambda i: (i, 0)),
                  pl.BlockSpec((BS_RET, RET_HEADS, RET_DK, RET_DV), lambda i: (i, 0, 0, 0)),
                  _const_spec(cos_s.shape), _const_spec(sin_s.shape), _const_spec(d64.shape),
                  _const_spec(qdec64.shape), _const_spec(kdec8.shape),
                  _const_spec((1, D_MODEL)),
                  pl.BlockSpec((n_qkv,) + win_b.shape[1:], lambda i: (BLK_Q[0] // n_qkv, 0, 0),
                               pipeline_mode=pl.Buffered(1))],
        out_specs=[pl.BlockSpec((rr, hdv), lambda i: (i, 0)),
                   pl.BlockSpec((BS_RET, RET_HEADS, RET_DK, RET_DV), lambda i: (i, 0, 0, 0))],
        out_shape=[jax.ShapeDtypeStruct((rows_s, hdv), F32),
                   jax.ShapeDtypeStruct((bsq, RET_HEADS, RET_DK, RET_DV), F32)],
        scratch_shapes=[pltpu.VMEM((rr, hdk), F32), pltpu.VMEM((rr, hdk), F32), pltpu.VMEM((rr, hdv), F32)],
        compiler_params=_params(("arbitrary",)),
        name="ret_sample",
    )(xs2, state_ret[0], cos_s, sin_s, d64, qdec64, kdec8, _row(g_mix[0]), win_b)

    seq_rows = lambda w: pl.BlockSpec((BS_MIX, w), lambda i: (i, 0))
    keep_l, keep_f = LRU_CONV - 1, FFN_CONV - 1
    x1_s, convst_s, hst_s = pl.pallas_call(
        _mixer_sample_kernel,
        grid=(bsq // BS_MIX,),
        in_specs=[seq_rows(ls * D_MODEL), seq_rows(ls * hdv), seq_rows(keep_l * D_LRU), seq_rows(D_LRU)]
                 + mixer_weight_specs,
        out_specs=[pl.BlockSpec((ls, BS_MIX, D_MODEL), lambda i: (0, i, 0)),
                   seq_rows(keep_l * D_LRU), seq_rows(D_LRU)],
        out_shape=[jax.ShapeDtypeStruct((ls, bsq, D_MODEL), F32),
                   jax.ShapeDtypeStruct((bsq, keep_l * D_LRU), F32),
                   jax.ShapeDtypeStruct((bsq, D_LRU), F32)],
        compiler_params=_params(("arbitrary",)),
        name="mixer_sample",
    )(x_sample.reshape(bsq, ls * D_MODEL), o_s.reshape(bsq, ls * hdv),
      state_lru_conv[0].reshape(bsq, keep_l * D_LRU), state_lru_h[0], *mixer_weights)

    y_s, ffnst_s = pl.pallas_call(
        functools.partial(_ffn_kernel, decode=True),
        grid=(bsq // BS_MIX,),
        in_specs=[pl.BlockSpec((ls, BS_MIX, D_MODEL), lambda i: (0, i, 0)),
                  seq_rows(ls * PLE_DIM), seq_rows(keep_f * 2 * D_FF)] + ffn_weight_specs,
        out_specs=[seq_rows(ls * D_MODEL), seq_rows(keep_f * 2 * D_FF)],
        out_shape=[jax.ShapeDtypeStruct((bsq, ls * D_MODEL), F32),
                   jax.ShapeDtypeStruct((bsq, keep_f * 2 * D_FF), F32)],
        compiler_params=_params(("arbitrary",)),
        name="ffn_sample",
    )(x1_s, p_sample[0].reshape(bsq, ls * PLE_DIM), state_ffn_conv[0].reshape(bsq, keep_f * 2 * D_FF),
      *ffn_weights)

    return (y_p,
            y_s.reshape(bsq, ls, D_MODEL),
            convst_p[None],
            hst_p.reshape(1, bp, D_LRU),
            sst_p.reshape(1, bp, RET_HEADS, RET_DK, RET_DV),
            ffnst_p[None],
            convst_s.reshape(1, bsq, keep_l, D_LRU),
            hst_s[None],
            sst_s[None],
            ffnst_s.reshape(1, bsq, keep_f, 2 * D_FF))
```

```python
import functools
import math

import jax
import jax.numpy as jnp
from jax import lax
from jax.experimental import pallas as pl
from jax.experimental.pallas import tpu as pltpu

F32 = jnp.float32
BF16 = jnp.bfloat16

D_MODEL = 1024
PLE_DIM = 256
D_LRU = 1024
LRU_BLOCK = 64
LRU_GROUP = 256
N_LRU_GROUPS = D_LRU // LRU_GROUP
LRU_CONV = 4
LRU_C = 8.0
RET_HEADS = 8
RET_DK = 64
RET_DV = 128
N_PAIRS = RET_HEADS // 2
ROPE_BASE = 10000.0
D_FF = 3072
FFN_CONV = 3
FFN_CHUNK = 512
EPS = 1e-6
PAST_LEN = 16384
SUBLANES = 8
LANES = 128

W_BLOCK = 512
BLK_LX, BLK_LG, BLK_Q, BLK_K, BLK_V, BLK_RG, BLK_GA, BLK_GB = (0, 2), (2, 2), (4, 1), (5, 1), (6, 2), (8, 2), (10, 2), (12, 2)

T_MIX = 256
T_FFN = 512
BS_RET = 32
BS_MIX = 64
VMEM_LIMIT = 56 * 1024 * 1024


def _sigmoid(x):
    return 1.0 / (1.0 + jnp.exp(-x))


def _gelu(x):
    c = math.sqrt(2.0 / math.pi)
    return x * (0.5 + 0.5 * jnp.tanh(x * (c + (c * 0.044715) * (x * x))))


def _rms(x, g):
    ms = jnp.mean(x * x, axis=-1, keepdims=True)
    return x * lax.rsqrt(ms + EPS) * g


def _dot(a, b):
    return jnp.dot(a, b, preferred_element_type=F32)


def _dot_blocks(a, w_refs, blocks=None):
    first, count = blocks if blocks is not None else (0, len(w_refs))
    parts = [_dot(a, w_refs[first + i][...]) for i in range(count)]
    return parts[0] if count == 1 else jnp.concatenate(parts, axis=1)


def _dot_nt(a, b):
    return lax.dot_general(a, b, (((1,), (1,)), ((), ())), preferred_element_type=F32)


def _dot_tn(a, b):
    return lax.dot_general(a, b, (((0,), (0,)), ((), ())), preferred_element_type=F32)


def _rotary_slab(x, cos, sin_signed):
    lane = lax.broadcasted_iota(jnp.int32, x.shape, x.ndim - 1)
    first_half = (lane & (RET_DK - 1)) < (RET_DK // 2)
    other = jnp.where(first_half,
                      pltpu.roll(x, LANES - RET_DK // 2, axis=x.ndim - 1),
                      pltpu.roll(x, RET_DK // 2, axis=x.ndim - 1))
    return x * cos + other * sin_signed


def _rows_from_lanes(ref, steps, width):
    return jnp.concatenate([ref[:, t * width:(t + 1) * width] for t in range(steps)], axis=0)


def _lru_gates(xc, wri_ref, br_ref, bi_ref, lam_ref):
    xcb = xc.astype(BF16)
    pre_r, pre_i = [], []
    for g in range(N_LRU_GROUPS):
        pre = _dot(xcb[:, g * LRU_GROUP:(g + 1) * LRU_GROUP], wri_ref[g])
        pre_r.append(pre[:, :LRU_GROUP])
        pre_i.append(pre[:, LRU_GROUP:])
    r = _sigmoid(jnp.concatenate(pre_r, axis=1) + br_ref[...])
    ig = _sigmoid(jnp.concatenate(pre_i, axis=1) + bi_ref[...])
    lam = lam_ref[...]
    softplus_neg_lam = jnp.maximum(-lam, 0.0) + jnp.log1p(jnp.exp(-jnp.abs(lam)))
    log_a = (-LRU_C * r) * softplus_neg_lam
    a = jnp.exp(log_a)
    u = jnp.sqrt(1.0 - a * a) * (ig * xc)
    return a, u


def _scan_within_groups(a3, u3):
    row = lax.broadcasted_iota(jnp.int32, (1,) + a3.shape[1:], 1)
    for s in (1, 2, 4):
        valid = row >= s
        a_prev = jnp.where(valid, pltpu.roll(a3, s, axis=1), 1.0)
        u_prev = jnp.where(valid, pltpu.roll(u3, s, axis=1), 0.0)
        u3 = a3 * u_prev + u3
        a3 = a3 * a_prev
    return a3, u3


def _head_norm_gate(o, rg, gng_ref, gnb_ref):
    parts = []
    for h in range(RET_HEADS):
        oh = o[:, h * RET_DV:(h + 1) * RET_DV]
        mu = jnp.mean(oh, axis=-1, keepdims=True)
        ctr = oh - mu
        var = jnp.mean(ctr * ctr, axis=-1, keepdims=True)
        parts.append(ctr * lax.rsqrt(var + EPS))
    y = jnp.concatenate(parts, axis=1) * gng_ref[...] + gnb_ref[...]
    return y * (rg * _sigmoid(rg))


def _merge_out(x, nxb, ya, yb, win_ref, wo_ref):
    ga = _dot_blocks(nxb, win_ref, BLK_GA)
    gb = _dot_blocks(nxb, win_ref, BLK_GB)
    merged = _sigmoid(ga) * ya + _sigmoid(gb) * yb
    return x + _dot_blocks(merged.astype(BF16), wo_ref)


def _mixer_prompt_kernel(x_ref, cos_ref, sin_ref, dmask_ref, qdec_ref, kdec_ref, cdec_ref,
                         gmix_ref, win_ref, wconv_ref, bconv_ref, wri_ref, br_ref, bi_ref, lam_ref,
                         wlo_ref, gng_ref, gnb_ref, wro_ref, wo_ref,
                         x1_ref, convst_ref, hst_ref, sst_ref,
                         z_ref, hc_ref, s_ref):
    T = T_MIX
    t = pl.program_id(1)

    @pl.when(t == 0)
    def _():
        z_ref[0:SUBLANES, :] = jnp.zeros((SUBLANES, D_LRU), F32)
        hc_ref[...] = jnp.zeros_like(hc_ref)
        s_ref[...] = jnp.zeros_like(s_ref)

    x = x_ref[...]
    nxb = _rms(x, gmix_ref[...]).astype(BF16)

    lx = _dot_blocks(nxb, win_ref, BLK_LX)
    z_ref[SUBLANES:SUBLANES + T, :] = lx
    xc = bconv_ref[...] + lx * wconv_ref[LRU_CONV - 1:LRU_CONV, :]
    for j in range(LRU_CONV - 1):
        start = SUBLANES - (LRU_CONV - 1) + j
        xc = xc + z_ref[start:start + T, :] * wconv_ref[j:j + 1, :]
    a, u = _lru_gates(xc, wri_ref, br_ref, bi_ref, lam_ref)
    G = T // SUBLANES
    A3, B3 = _scan_within_groups(a.reshape(G, SUBLANES, D_LRU), u.reshape(G, SUBLANES, D_LRU))
    h_in = hc_ref[...]
    hs_groups = []
    for g in range(G):
        h = A3[g] * h_in + B3[g]
        hs_groups.append(h)
        h_in = jnp.broadcast_to(h[SUBLANES - 1:SUBLANES, :], (SUBLANES, D_LRU))
    h_last = h_in
    hc_ref[...] = h_last
    hs = jnp.concatenate(hs_groups, axis=0)
    lg = _dot_blocks(nxb, win_ref, BLK_LG)
    ya = _dot_blocks((hs * _gelu(lg)).astype(BF16), wlo_ref)

    q = _dot_blocks(nxb, win_ref, BLK_Q)
    k = _dot_blocks(nxb, win_ref, BLK_K)
    v = _dot_blocks(nxb, win_ref, BLK_V)
    cos = cos_ref[...]
    sin_signed = sin_ref[...]
    lane = lax.broadcasted_iota(jnp.int32, (T, LANES), 1)
    o_parts = []
    for j in range(N_PAIRS):
        sl = slice(j * LANES, (j + 1) * LANES)
        qr = _rotary_slab(q[:, sl], cos, sin_signed)
        kr = _rotary_slab(k[:, sl], cos, sin_signed) * (RET_DK ** -0.5)
        kb = kr.astype(BF16)
        s_pair = s_ref[j]
        s_pair_b = s_pair.astype(BF16)
        for half in range(2):
            h = 2 * j + half
            qm = jnp.where((lane >> 6) == half, qr, 0.0).astype(BF16)
            p = (_dot_nt(qm, kb) * dmask_ref[h]).astype(BF16)
            vb = v[:, h * RET_DV:(h + 1) * RET_DV].astype(BF16)
            o_parts.append(_dot(p, vb) + _dot(qm, s_pair_b) * qdec_ref[:, h * RET_DV:(h + 1) * RET_DV])
        kd = (kr * kdec_ref[:, sl]).astype(BF16)
        upd = _dot_tn(kd, v[:, 2 * j * RET_DV:(2 * j + 2) * RET_DV].astype(BF16))
        s_ref[j] = s_pair * cdec_ref[j] + jnp.concatenate(
            [upd[0:RET_DK, 0:RET_DV], upd[RET_DK:2 * RET_DK, RET_DV:2 * RET_DV]], axis=0)
    o = jnp.concatenate(o_parts, axis=1)
    rg = _dot_blocks(nxb, win_ref, BLK_RG)
    yb = _dot_blocks(_head_norm_gate(o, rg, gng_ref, gnb_ref).astype(BF16), wro_ref)

    x1_ref[...] = _merge_out(x, nxb, ya, yb, win_ref, wo_ref)

    z_ref[0:SUBLANES, :] = z_ref[T:T + SUBLANES, :]

    @pl.when(t == pl.num_programs(1) - 1)
    def _():
        convst_ref[...] = z_ref[SUBLANES - (LRU_CONV - 1):SUBLANES, :]
        hst_ref[...] = h_last[0:1, :]
        sst_ref[...] = s_ref[...]


def _ret_sample_kernel(x_ref, st_ref, cos_ref, sin_ref, d64_ref, qdec_ref, kdec_ref,
                       gmix_ref, wqkv_ref,
                       o_ref, so_ref,
                       q_ref, k_ref, v_ref):
    bs = BS_RET
    nq = RET_HEADS * RET_DK
    nxb = _rms(x_ref[...], gmix_ref[...]).astype(BF16)
    qkv = _dot_blocks(nxb, wqkv_ref)
    cos = cos_ref[...][None]
    sin_signed = sin_ref[...][None]
    for j in range(N_PAIRS):
        sl = slice(j * LANES, (j + 1) * LANES)
        q3 = qkv[:, j * LANES:(j + 1) * LANES].reshape(bs, SUBLANES, LANES)
        k3 = qkv[:, nq + j * LANES:nq + (j + 1) * LANES].reshape(bs, SUBLANES, LANES)
        q_ref[:, sl] = _rotary_slab(q3, cos, sin_signed).reshape(bs * SUBLANES, LANES)
        k_ref[:, sl] = (_rotary_slab(k3, cos, sin_signed) * (RET_DK ** -0.5)).reshape(bs * SUBLANES, LANES)
    v_ref[...] = qkv[:, 2 * nq:]

    lane = lax.broadcasted_iota(jnp.int32, (SUBLANES, nq), 1)
    gammas = [1.0 - 2.0 ** (-5.0 - h) for h in range(RET_HEADS)]

    def per_seq(s, carry):
        r0 = pl.multiple_of(s * SUBLANES, SUBLANES)
        qs = q_ref[pl.ds(r0, SUBLANES), :]
        ks = k_ref[pl.ds(r0, SUBLANES), :]
        vs = v_ref[pl.ds(r0, SUBLANES), :]
        vb = vs.astype(BF16)
        s0 = st_ref[s]
        qbd = jnp.concatenate([jnp.where((lane >> 6) == h, qs, 0.0) for h in range(RET_HEADS)],
                              axis=0).astype(BF16)
        o_state = _dot(qbd, s0.reshape(RET_HEADS * RET_DK, RET_DV).astype(BF16)) * qdec_ref[...]
        p = (_dot_nt(qbd, ks.astype(BF16)) * d64_ref[...]).astype(BF16)
        o_intra = _dot(p, vb)
        o_ref[pl.ds(r0, SUBLANES), :] = jnp.concatenate(
            [o_intra[h * SUBLANES:(h + 1) * SUBLANES, h * RET_DV:(h + 1) * RET_DV]
             + o_state[h * SUBLANES:(h + 1) * SUBLANES, :] for h in range(RET_HEADS)], axis=1)
        kd = (ks * kdec_ref[...]).astype(BF16)
        for j in range(N_PAIRS):
            upd = _dot_tn(kd[:, j * LANES:(j + 1) * LANES], vb[:, 2 * j * RET_DV:(2 * j + 2) * RET_DV])
            so_ref[s, 2 * j] = s0[2 * j] * (gammas[2 * j] ** SUBLANES) + upd[0:RET_DK, 0:RET_DV]
            so_ref[s, 2 * j + 1] = (s0[2 * j + 1] * (gammas[2 * j + 1] ** SUBLANES)
                                    + upd[RET_DK:2 * RET_DK, RET_DV:2 * RET_DV])
        return carry

    lax.fori_loop(0, bs, per_seq, 0)


def _mixer_sample_kernel(x_ref, o_ref, cst_ref, h0_ref,
                         gmix_ref, win_ref, wconv_ref, bconv_ref, wri_ref, br_ref, bi_ref, lam_ref,
                         wlo_ref, gng_ref, gnb_ref, wro_ref, wo_ref,
                         x1_ref, convst_ref, hst_ref):
    bs = BS_MIX
    steps = SUBLANES
    x = _rows_from_lanes(x_ref, steps, D_MODEL)
    nxb = _rms(x, gmix_ref[...]).astype(BF16)

    lx = _dot_blocks(nxb, win_ref, BLK_LX)
    keep = LRU_CONV - 1
    ext = ([cst_ref[:, j * D_LRU:(j + 1) * D_LRU] for j in range(keep)]
           + [lx[t * bs:(t + 1) * bs] for t in range(steps)])
    xc = jnp.concatenate(
        [bconv_ref[...] + sum(ext[t + j] * wconv_ref[j:j + 1, :] for j in range(LRU_CONV)) for t in range(steps)],
        axis=0)
    for j in range(keep):
        convst_ref[:, j * D_LRU:(j + 1) * D_LRU] = ext[steps + j]
    a, u = _lru_gates(xc, wri_ref, br_ref, bi_ref, lam_ref)
    h = h0_ref[...]
    hs = []
    for t in range(steps):
        h = a[t * bs:(t + 1) * bs] * h + u[t * bs:(t + 1) * bs]
        hs.append(h)
    hst_ref[...] = h
    lg = _dot_blocks(nxb, win_ref, BLK_LG)
    ya = _dot_blocks((jnp.concatenate(hs, axis=0) * _gelu(lg)).astype(BF16), wlo_ref)

    rg = _dot_blocks(nxb, win_ref, BLK_RG)
    o = _rows_from_lanes(o_ref, steps, RET_HEADS * RET_DV)
    yb = _dot_blocks(_head_norm_gate(o, rg, gng_ref, gnb_ref).astype(BF16), wro_ref)

    x1_ref[...] = _merge_out(x, nxb, ya, yb, win_ref, wo_ref).reshape(steps, bs, D_MODEL)


def _ffn_kernel(x_ref, p_ref, *rest, decode):
    keep = FFN_CONV - 1
    if decode:
        (cst_ref, gffn_ref, wup_ref, wconv_ref, bconv_ref, wdown_ref, wple_ref, gple_ref, wgate_ref, gfin_ref,
         y_ref, convst_ref) = rest
        steps, bs = x_ref.shape[0], x_ref.shape[1]
        x = x_ref[...].reshape(steps * bs, D_MODEL)
        p = _rows_from_lanes(p_ref, steps, PLE_DIM)
    else:
        (gffn_ref, wup_ref, wconv_ref, bconv_ref, wdown_ref, wple_ref, gple_ref, wgate_ref, gfin_ref,
         y_ref, convst_ref, z_ref, carry_ref) = rest
        t = pl.program_id(1)

        @pl.when(t == 0)
        def _():
            carry_ref[...] = jnp.zeros_like(carry_ref)

        x = x_ref[...]
        p = p_ref[...]
    rows = x.shape[0]
    nxb = _rms(x, gffn_ref[...]).astype(BF16)
    n_chunks = D_FF // FFN_CHUNK
    act_chunks = []
    for c in range(n_chunks):
        halves = []
        for half in range(2):
            blk = half * n_chunks + c
            cols = slice(blk * FFN_CHUNK, (blk + 1) * FFN_CHUNK)
            up = _dot(nxb, wup_ref[blk][...])
            w = wconv_ref[:, cols]
            bias = bconv_ref[:, cols]
            if decode:
                ext = ([cst_ref[:, j * 2 * D_FF + cols.start:j * 2 * D_FF + cols.stop] for j in range(keep)]
                       + [up[s * bs:(s + 1) * bs] for s in range(steps)])
                halves.append(jnp.concatenate(
                    [bias + sum(ext[s + j] * w[j:j + 1] for j in range(FFN_CONV)) for s in range(steps)], axis=0))
                for j in range(keep):
                    convst_ref[:, j * 2 * D_FF + cols.start:j * 2 * D_FF + cols.stop] = ext[steps + j]
            else:
                zc_ref = z_ref.at[2 * (c % 2) + half]
                zc_ref[0:SUBLANES, :] = carry_ref[:, cols]
                zc_ref[SUBLANES:SUBLANES + rows, :] = up
                y = bias + up * w[keep:keep + 1]
                for j in range(keep):
                    start = SUBLANES - keep + j
                    y = y + zc_ref[start:start + rows, :] * w[j:j + 1]
                carry_ref[:, cols] = zc_ref[rows:rows + SUBLANES, :]
                halves.append(y)
        act_chunks.append((_gelu(halves[0]) * halves[1]).astype(BF16))
    x2 = x + _dot_blocks(jnp.concatenate(act_chunks, axis=1), wdown_ref)

    e = _rms(_dot_blocks(p.astype(BF16), wple_ref), gple_ref[...])
    x3 = x2 + _sigmoid(_dot_blocks(x2.astype(BF16), wgate_ref)) * e
    y = _rms(x3, gfin_ref[...])

    if decode:
        for s in range(steps):
            y_ref[:, s * D_MODEL:(s + 1) * D_MODEL] = y[s * bs:(s + 1) * bs]
    else:
        y_ref[...] = y

        @pl.when(t == pl.num_programs(1) - 1)
        def _():
            convst_ref[...] = carry_ref[SUBLANES - keep:SUBLANES, :]


def _const_spec(shape):
    n = len(shape)
    return pl.BlockSpec(shape, lambda *_: (0,) * n, pipeline_mode=pl.Buffered(1))


def _row(v):
    return v.reshape(1, -1)


def _rotary_tables(pos):
    half = RET_DK // 2
    inv = ROPE_BASE ** (-jnp.arange(half, dtype=F32) / half)
    ang = pos.astype(F32)[:, None] * inv[None, :]
    cos, sin = jnp.cos(ang), jnp.sin(ang)
    return jnp.tile(cos, (1, LANES // half)), jnp.tile(jnp.concatenate([-sin, sin], axis=1), (1, LANES // RET_DK))


def _decay_tables(chunk):
    log_gamma = jnp.log1p(-(2.0 ** (-5.0 - jnp.arange(RET_HEADS, dtype=F32))))
    idx = jnp.arange(chunk)
    rel = idx[:, None] - idx[None, :]
    dmask = jnp.where(rel[None] >= 0,
                      jnp.exp(log_gamma[:, None, None] * jnp.maximum(rel, 0).astype(F32)[None]), 0.0)
    q_decay = jnp.exp(log_gamma[None, :] * (idx + 1).astype(F32)[:, None])
    k_decay = jnp.exp(log_gamma[None, :] * (chunk - 1 - idx).astype(F32)[:, None])
    chunk_decay = jnp.exp(log_gamma * chunk)
    return dmask, q_decay, k_decay, chunk_decay


def _block_diag_gates(w_r, w_i):
    per = LRU_GROUP // LRU_BLOCK
    eye = jnp.eye(per, dtype=w_r.dtype)

    def pack(w):
        w4 = w.reshape(N_LRU_GROUPS, per, LRU_BLOCK, LRU_BLOCK)
        return jnp.einsum('gbij,bc->gbicj', w4, eye).reshape(N_LRU_GROUPS, LRU_GROUP, LRU_GROUP)

    return jnp.concatenate([pack(w_r), pack(w_i)], axis=-1).astype(BF16)


def _whole(a):
    return ([a], [_const_spec(a.shape)], None)


def _col_blocks(w, first=0, count=None):
    k, n = w.shape
    count = n // W_BLOCK - first if count is None else count
    wb = w.astype(BF16)
    specs = [pl.BlockSpec((k, W_BLOCK), functools.partial(lambda j, *_: (0, j), first + i),
                          pipeline_mode=pl.Buffered(1)) for i in range(count)]
    return ([wb] * count, specs, count)


def _flatten(groups):
    arrays = [a for g in groups for a in g[0]]
    specs = [sp for g in groups for sp in g[1]]
    return arrays, specs, [g[2] for g in groups]


def _grouped(body, counts):
    def kernel_fn(*refs):
        args, i = [], 0
        for n in counts:
            if n is None:
                args.append(refs[i])
                i += 1
            else:
                args.append(list(refs[i:i + n]))
                i += n
        assert i == len(refs)
        return body(*args)
    return kernel_fn


def _call(body, name, grid, data, weights, out_specs, out_shape, scratch_shapes=()):
    arrays, specs, counts = _flatten([([a], [sp], None) for a, sp in data] + list(weights))
    counts = counts + [None] * (len(out_shape) + len(scratch_shapes))
    return pl.pallas_call(
        _grouped(body, counts), grid=grid, in_specs=specs, out_specs=out_specs, out_shape=out_shape,
        scratch_shapes=list(scratch_shapes),
        compiler_params=pltpu.CompilerParams(dimension_semantics=("arbitrary",) * len(grid),
                                             vmem_limit_bytes=VMEM_LIMIT),
        name=name)(*arrays)


def kernel(x_prompt, x_sample, p_prompt, p_sample, state_lru_conv, state_lru_h, state_ret, state_ffn_conv, g_mix, w_in, w_lru_conv, b_lru_conv, w_r, b_r, w_i, b_i, lru_lambda, w_lru_out, gn_g, gn_b, w_ret_out, w_o, g_ffn, w_up, w_ffn_conv, b_ffn_conv, w_down, w_ple, g_ple, w_ple_gate, g_final):
    depth = g_mix.shape[0]
    assert depth == 1
    bp, lp, _ = x_prompt.shape
    bsq, ls, _ = x_sample.shape
    assert ls == SUBLANES and lp % T_MIX == 0 and lp % T_FFN == 0
    assert bsq % BS_RET == 0 and bsq % BS_MIX == 0
    hdv = RET_HEADS * RET_DV
    hdk = RET_HEADS * RET_DK

    mixer_weights = [_whole(_row(g_mix[0])), _col_blocks(w_in[0]), _whole(w_lru_conv[0]), _whole(_row(b_lru_conv[0])),
                     _whole(_block_diag_gates(w_r[0], w_i[0])), _whole(_row(b_r[0])), _whole(_row(b_i[0])),
                     _whole(_row(lru_lambda[0])), _col_blocks(w_lru_out[0]), _whole(_row(gn_g[0])),
                     _whole(_row(gn_b[0])), _col_blocks(w_ret_out[0]), _col_blocks(w_o[0])]
    ffn_weights = [_whole(_row(g_ffn[0])), _col_blocks(w_up[0]), _whole(w_ffn_conv[0]), _whole(_row(b_ffn_conv[0])),
                   _col_blocks(w_down[0]), _col_blocks(w_ple[0]), _whole(_row(g_ple[0])),
                   _col_blocks(w_ple_gate[0]), _whole(_row(g_final))]

    cos_p, sin_p = _rotary_tables(jnp.arange(lp, dtype=jnp.int32))
    dmask, q_decay, k_decay, chunk_decay = _decay_tables(T_MIX)
    qdec_p = jnp.repeat(q_decay, RET_DV, axis=1)
    kdec_p = jnp.repeat(k_decay, RET_DK, axis=1)
    cdec_p = jnp.broadcast_to(jnp.repeat(chunk_decay, RET_DK).reshape(N_PAIRS, 2 * RET_DK, 1),
                              (N_PAIRS, 2 * RET_DK, RET_DV))
    tile = lambda w: pl.BlockSpec((None, T_MIX, w), lambda b, t: (b, t, 0))
    table = pl.BlockSpec((T_MIX, LANES), lambda b, t: (t, 0))
    x1_p, convst_p, hst_p, sst_p = _call(
        _mixer_prompt_kernel, "mixer_prompt", (bp, lp // T_MIX),
        data=[(x_prompt, tile(D_MODEL)), (cos_p, table), (sin_p, table)],
        weights=[_whole(dmask), _whole(qdec_p), _whole(kdec_p), _whole(cdec_p)] + mixer_weights,
        out_specs=[tile(D_MODEL),
                   pl.BlockSpec((None, LRU_CONV - 1, D_LRU), lambda b, t: (b, 0, 0)),
                   pl.BlockSpec((None, 1, D_LRU), lambda b, t: (b, 0, 0)),
                   pl.BlockSpec((None, N_PAIRS, 2 * RET_DK, RET_DV), lambda b, t: (b, 0, 0, 0))],
        out_shape=[jax.ShapeDtypeStruct((bp, lp, D_MODEL), F32),
                   jax.ShapeDtypeStruct((bp, LRU_CONV - 1, D_LRU), F32),
                   jax.ShapeDtypeStruct((bp, 1, D_LRU), F32),
                   jax.ShapeDtypeStruct((bp, N_PAIRS, 2 * RET_DK, RET_DV), F32)],
        scratch_shapes=[pltpu.VMEM((T_MIX + 2 * SUBLANES, D_LRU), F32), pltpu.VMEM((SUBLANES, D_LRU), F32),
                        pltpu.VMEM((N_PAIRS, 2 * RET_DK, RET_DV), F32)])

    tilef = lambda w: pl.BlockSpec((None, T_FFN, w), lambda b, t: (b, t, 0))
    y_p, ffnst_p = _call(
        functools.partial(_ffn_kernel, decode=False), "ffn_prompt", (bp, lp // T_FFN),
        data=[(x1_p, tilef(D_MODEL)), (p_prompt[0], tilef(PLE_DIM))],
        weights=ffn_weights,
        out_specs=[tilef(D_MODEL), pl.BlockSpec((None, FFN_CONV - 1, 2 * D_FF), lambda b, t: (b, 0, 0))],
        out_shape=[jax.ShapeDtypeStruct((bp, lp, D_MODEL), F32),
                   jax.ShapeDtypeStruct((bp, FFN_CONV - 1, 2 * D_FF), F32)],
        scratch_shapes=[pltpu.VMEM((4, T_FFN + 2 * SUBLANES, FFN_CHUNK), F32),
                        pltpu.VMEM((SUBLANES, 2 * D_FF), F32)])

    rows_s = bsq * ls
    cos_s, sin_s = _rotary_tables(PAST_LEN + jnp.arange(ls, dtype=jnp.int32))
    dmask8, q_decay8, k_decay8, _ = _decay_tables(ls)
    d64 = dmask8.reshape(RET_HEADS * ls, ls)
    qdec64 = jnp.broadcast_to(q_decay8.T.reshape(RET_HEADS * ls, 1), (RET_HEADS * ls, RET_DV))
    kdec8 = jnp.repeat(k_decay8, RET_DK, axis=1)
    rr = BS_RET * ls
    state_block = pl.BlockSpec((BS_RET, RET_HEADS, RET_DK, RET_DV), lambda i: (i, 0, 0, 0))
    o_s, sst_s = _call(
        _ret_sample_kernel, "ret_sample", (bsq // BS_RET,),
        data=[(x_sample.reshape(rows_s, D_MODEL), pl.BlockSpec((rr, D_MODEL), lambda i: (i, 0))),
              (state_ret[0], state_block)],
        weights=[_whole(cos_s), _whole(sin_s), _whole(d64), _whole(qdec64), _whole(kdec8), _whole(_row(g_mix[0])),
                 _col_blocks(w_in[0], BLK_Q[0], BLK_RG[0] - BLK_Q[0])],
        out_specs=[pl.BlockSpec((rr, hdv), lambda i: (i, 0)), state_block],
        out_shape=[jax.ShapeDtypeStruct((rows_s, hdv), F32),
                   jax.ShapeDtypeStruct((bsq, RET_HEADS, RET_DK, RET_DV), F32)],
        scratch_shapes=[pltpu.VMEM((rr, hdk), F32), pltpu.VMEM((rr, hdk), F32), pltpu.VMEM((rr, hdv), F32)])

    seq_rows = lambda w: pl.BlockSpec((BS_MIX, w), lambda i: (i, 0))
    token_major = pl.BlockSpec((ls, BS_MIX, D_MODEL), lambda i: (0, i, 0))
    keep_l, keep_f = LRU_CONV - 1, FFN_CONV - 1
    x1_s, convst_s, hst_s = _call(
        _mixer_sample_kernel, "mixer_sample", (bsq // BS_MIX,),
        data=[(x_sample.reshape(bsq, ls * D_MODEL), seq_rows(ls * D_MODEL)),
              (o_s.reshape(bsq, ls * hdv), seq_rows(ls * hdv)),
              (state_lru_conv[0].reshape(bsq, keep_l * D_LRU), seq_rows(keep_l * D_LRU)),
              (state_lru_h[0], seq_rows(D_LRU))],
        weights=mixer_weights,
        out_specs=[token_major, seq_rows(keep_l * D_LRU), seq_rows(D_LRU)],
        out_shape=[jax.ShapeDtypeStruct((ls, bsq, D_MODEL), F32),
                   jax.ShapeDtypeStruct((bsq, keep_l * D_LRU), F32),
                   jax.ShapeDtypeStruct((bsq, D_LRU), F32)])

    y_s, ffnst_s = _call(
        functools.partial(_ffn_kernel, decode=True), "ffn_sample", (bsq // BS_MIX,),
        data=[(x1_s, token_major),
              (p_sample[0].reshape(bsq, ls * PLE_DIM), seq_rows(ls * PLE_DIM)),
              (state_ffn_conv[0].reshape(bsq, keep_f * 2 * D_FF), seq_rows(keep_f * 2 * D_FF))],
        weights=ffn_weights,
        out_specs=[seq_rows(ls * D_MODEL), seq_rows(keep_f * 2 * D_FF)],
        out_shape=[jax.ShapeDtypeStruct((bsq, ls * D_MODEL), F32),
                   jax.ShapeDtypeStruct((bsq, keep_f * 2 * D_FF), F32)])

    return (y_p,
            y_s.reshape(bsq, ls, D_MODEL),
            convst_p[None],
            hst_p.reshape(1, bp, D_LRU),
            sst_p.reshape(1, bp, RET_HEADS, RET_DK, RET_DV),
            ffnst_p[None],
            convst_s.reshape(1, bsq, keep_l, D_LRU),
            hst_s[None],
            sst_s[None],
            ffnst_s.reshape(1, bsq, keep_f, 2 * D_FF))
```

```python
import functools
import math

import jax
import jax.numpy as jnp
import numpy as np
from jax import lax
from jax.experimental import pallas as pl
from jax.experimental.pallas import tpu as pltpu

F32 = jnp.float32
BF16 = jnp.bfloat16

D_MODEL = 1024
PLE_DIM = 256
D_LRU = 1024
LRU_BLOCK = 64
LRU_GROUP = 256
N_LRU_GROUPS = D_LRU // LRU_GROUP
LRU_CONV = 4
LRU_C = 8.0
RET_HEADS = 8
RET_DK = 64
RET_DV = 128
N_PAIRS = RET_HEADS // 2
ROPE_BASE = 10000.0
D_FF = 3072
FFN_CONV = 3
FFN_CHUNK = 512
EPS = 1e-6
PAST_LEN = 16384
SUBLANES = 8
LANES = 128

W_BLOCK = 512
BLK_LX, BLK_LG, BLK_Q, BLK_K, BLK_V, BLK_RG, BLK_GA, BLK_GB = (0, 2), (2, 2), (4, 1), (5, 1), (6, 2), (8, 2), (10, 2), (12, 2)

T_MIX = 256
T_FFN = 512
BS_RET = 32
BS_MIX = 64
VMEM_LIMIT = 56 * 1024 * 1024


def _sigmoid(x):
    return 1.0 / (1.0 + jnp.exp(-x))


def _gelu(x):
    c = math.sqrt(2.0 / math.pi)
    return x * (0.5 + 0.5 * jnp.tanh(x * (c + (c * 0.044715) * (x * x))))


def _rms(x, g):
    ms = jnp.mean(x * x, axis=-1, keepdims=True)
    return x * lax.rsqrt(ms + EPS) * g


def _dot(a, b):
    return jnp.dot(a, b, preferred_element_type=F32)


def _dot_blocks(a, w_refs, blocks=None):
    first, count = blocks if blocks is not None else (0, len(w_refs))
    parts = [_dot(a, w_refs[first + i][...]) for i in range(count)]
    return parts[0] if count == 1 else jnp.concatenate(parts, axis=1)


def _dot_nt(a, b):
    return lax.dot_general(a, b, (((1,), (1,)), ((), ())), preferred_element_type=F32)


def _dot_tn(a, b):
    return lax.dot_general(a, b, (((0,), (0,)), ((), ())), preferred_element_type=F32)


def _rotary_slab(x, cos, sin_signed):
    lane = lax.broadcasted_iota(jnp.int32, x.shape, x.ndim - 1)
    first_half = (lane & (RET_DK - 1)) < (RET_DK // 2)
    other = jnp.where(first_half,
                      pltpu.roll(x, LANES - RET_DK // 2, axis=x.ndim - 1),
                      pltpu.roll(x, RET_DK // 2, axis=x.ndim - 1))
    return x * cos + other * sin_signed


def _token_major(ref):
    return jnp.concatenate([ref[:, t, :] for t in range(ref.shape[1])], axis=0)


def _gate_preacts(xc, wri_ref):
    xcb = xc.astype(BF16)
    return [_dot(xcb[:, g * LRU_GROUP:(g + 1) * LRU_GROUP], wri_ref[g]) for g in range(N_LRU_GROUPS)]


def _lru_gates(xc, pre, br_ref, bi_ref, lam_ref):
    pre_r = [p[:, :LRU_GROUP] for p in pre]
    pre_i = [p[:, LRU_GROUP:] for p in pre]
    r = _sigmoid(jnp.concatenate(pre_r, axis=1) + br_ref[...])
    ig = _sigmoid(jnp.concatenate(pre_i, axis=1) + bi_ref[...])
    lam = lam_ref[...]
    softplus_neg_lam = jnp.maximum(-lam, 0.0) + jnp.log1p(jnp.exp(-jnp.abs(lam)))
    log_a = (-LRU_C * r) * softplus_neg_lam
    a = jnp.exp(log_a)
    u = jnp.sqrt(1.0 - a * a) * (ig * xc)
    return a, u


def _scan_within_groups(a3, u3):
    row = lax.broadcasted_iota(jnp.int32, (1,) + a3.shape[1:], 1)
    for s in (1, 2, 4):
        valid = row >= s
        a_prev = jnp.where(valid, pltpu.roll(a3, s, axis=1), 1.0)
        u_prev = jnp.where(valid, pltpu.roll(u3, s, axis=1), 0.0)
        u3 = a3 * u_prev + u3
        a3 = a3 * a_prev
    return a3, u3


def _head_norm_gate(o, rg, gng_ref, gnb_ref):
    parts = []
    for h in range(RET_HEADS):
        oh = o[:, h * RET_DV:(h + 1) * RET_DV]
        mu = jnp.mean(oh, axis=-1, keepdims=True)
        ctr = oh - mu
        var = jnp.mean(ctr * ctr, axis=-1, keepdims=True)
        parts.append(ctr * lax.rsqrt(var + EPS))
    y = jnp.concatenate(parts, axis=1) * gng_ref[...] + gnb_ref[...]
    return y * (rg * _sigmoid(rg))


def _merge_out(x, nxb, ya, yb, win_ref, wo_ref):
    ga = _dot_blocks(nxb, win_ref, BLK_GA)
    gb = _dot_blocks(nxb, win_ref, BLK_GB)
    merged = _sigmoid(ga) * ya + _sigmoid(gb) * yb
    return x + _dot_blocks(merged.astype(BF16), wo_ref)


def _mixer_prompt_kernel(x_ref, cos_ref, sin_ref, dmask_ref, qdec_ref, kdec_ref, cdec_ref,
                         gmix_ref, win_ref, wconv_ref, bconv_ref, wri_ref, br_ref, bi_ref, lam_ref,
                         wlo_ref, gng_ref, gnb_ref, wro_ref, wo_ref,
                         x1_ref, convst_ref, hst_ref, sst_ref,
                         proj_ref, hc_ref, s_ref):
    T = T_MIX
    t = pl.program_id(1)

    @pl.when(t == 0)
    def _():
        proj_ref[0:SUBLANES, :] = jnp.zeros((SUBLANES, proj_ref.shape[1]), F32)
        hc_ref[...] = jnp.zeros_like(hc_ref)
        s_ref[...] = jnp.zeros_like(s_ref)

    x = x_ref[...]
    nxb = _rms(x, gmix_ref[...]).astype(BF16)

    def project(first, count):
        for blk in range(first, first + count):
            proj_ref[SUBLANES:SUBLANES + T, blk * W_BLOCK:(blk + 1) * W_BLOCK] = _dot(nxb, win_ref[blk][...])

    project(0, BLK_Q[0])

    def proj(blocks):
        first, count = blocks
        return proj_ref[SUBLANES:SUBLANES + T, first * W_BLOCK:(first + count) * W_BLOCK]

    lx_cols = slice(BLK_LX[0] * W_BLOCK, (BLK_LX[0] + BLK_LX[1]) * W_BLOCK)
    xc = bconv_ref[...] + proj(BLK_LX) * wconv_ref[LRU_CONV - 1:LRU_CONV, :]
    for j in range(LRU_CONV - 1):
        start = SUBLANES - (LRU_CONV - 1) + j
        xc = xc + proj_ref[start:start + T, lx_cols] * wconv_ref[j:j + 1, :]
    pre = _gate_preacts(xc, wri_ref)
    project(BLK_Q[0], len(win_ref) - BLK_Q[0])
    a, u = _lru_gates(xc, pre, br_ref, bi_ref, lam_ref)
    G = T // SUBLANES
    A3, B3 = _scan_within_groups(a.reshape(G, SUBLANES, D_LRU), u.reshape(G, SUBLANES, D_LRU))
    h_in = hc_ref[...]
    hs_groups = []
    for g in range(G):
        h = A3[g] * h_in + B3[g]
        hs_groups.append(h)
        h_in = jnp.broadcast_to(h[SUBLANES - 1:SUBLANES, :], (SUBLANES, D_LRU))
    h_last = h_in
    hc_ref[...] = h_last
    hs = jnp.concatenate(hs_groups, axis=0)
    ya = _dot_blocks((hs * _gelu(proj(BLK_LG))).astype(BF16), wlo_ref)

    q = proj(BLK_Q)
    k = proj(BLK_K)
    v = proj(BLK_V)
    cos = cos_ref[...]
    sin_signed = sin_ref[...]
    lane = lax.broadcasted_iota(jnp.int32, (T, LANES), 1)
    o_parts = []
    for j in range(N_PAIRS):
        sl = slice(j * LANES, (j + 1) * LANES)
        qr = _rotary_slab(q[:, sl], cos, sin_signed)
        kr = _rotary_slab(k[:, sl], cos, sin_signed) * (RET_DK ** -0.5)
        kb = kr.astype(BF16)
        s_pair = s_ref[j]
        s_pair_b = s_pair.astype(BF16)
        for half in range(2):
            h = 2 * j + half
            qm = jnp.where((lane >> 6) == half, qr, 0.0).astype(BF16)
            p = (_dot_nt(qm, kb) * dmask_ref[h]).astype(BF16)
            vb = v[:, h * RET_DV:(h + 1) * RET_DV].astype(BF16)
            o_parts.append(_dot(p, vb) + _dot(qm, s_pair_b) * qdec_ref[:, h * RET_DV:(h + 1) * RET_DV])
        kd = (kr * kdec_ref[:, sl]).astype(BF16)
        upd = _dot_tn(kd, v[:, 2 * j * RET_DV:(2 * j + 2) * RET_DV].astype(BF16))
        s_ref[j] = s_pair * cdec_ref[j] + jnp.concatenate(
            [upd[0:RET_DK, 0:RET_DV], upd[RET_DK:2 * RET_DK, RET_DV:2 * RET_DV]], axis=0)
    o = jnp.concatenate(o_parts, axis=1)
    yb = _dot_blocks(_head_norm_gate(o, proj(BLK_RG), gng_ref, gnb_ref).astype(BF16), wro_ref)

    merged = _sigmoid(proj(BLK_GA)) * ya + _sigmoid(proj(BLK_GB)) * yb
    x1_ref[...] = x + _dot_blocks(merged.astype(BF16), wo_ref)

    proj_ref[0:SUBLANES, lx_cols] = proj_ref[T:T + SUBLANES, lx_cols]

    @pl.when(t == pl.num_programs(1) - 1)
    def _():
        convst_ref[...] = proj_ref[SUBLANES - (LRU_CONV - 1):SUBLANES, lx_cols]
        hst_ref[...] = h_last[0:1, :]
        sst_ref[...] = s_ref[...]


def _ret_sample_kernel(x_ref, st_ref, cos_ref, sin_ref, d64_ref, qdec_ref, kdec_ref,
                       gmix_ref, wqkv_ref,
                       o_ref, so_ref,
                       q_ref, k_ref, v_ref):
    bs = BS_RET
    nq = RET_HEADS * RET_DK
    nxb = _rms(x_ref[...], gmix_ref[...]).astype(BF16)
    qkv = _dot_blocks(nxb, wqkv_ref)
    cos = cos_ref[...][None]
    sin_signed = sin_ref[...][None]
    for j in range(N_PAIRS):
        sl = slice(j * LANES, (j + 1) * LANES)
        q3 = qkv[:, j * LANES:(j + 1) * LANES].reshape(bs, SUBLANES, LANES)
        k3 = qkv[:, nq + j * LANES:nq + (j + 1) * LANES].reshape(bs, SUBLANES, LANES)
        q_ref[:, sl] = _rotary_slab(q3, cos, sin_signed).reshape(bs * SUBLANES, LANES)
        k_ref[:, sl] = (_rotary_slab(k3, cos, sin_signed) * (RET_DK ** -0.5)).reshape(bs * SUBLANES, LANES)
    v_ref[...] = qkv[:, 2 * nq:]

    lane = lax.broadcasted_iota(jnp.int32, (SUBLANES, nq), 1)
    gammas = [1.0 - 2.0 ** (-5.0 - h) for h in range(RET_HEADS)]

    def per_seq(s, carry):
        r0 = pl.multiple_of(s * SUBLANES, SUBLANES)
        qs = q_ref[pl.ds(r0, SUBLANES), :]
        ks = k_ref[pl.ds(r0, SUBLANES), :]
        vs = v_ref[pl.ds(r0, SUBLANES), :]
        vb = vs.astype(BF16)
        s0 = st_ref[s]
        qbd = jnp.concatenate([jnp.where((lane >> 6) == h, qs, 0.0) for h in range(RET_HEADS)],
                              axis=0).astype(BF16)
        o_state = _dot(qbd, s0.reshape(RET_HEADS * RET_DK, RET_DV).astype(BF16)) * qdec_ref[...]
        p = (_dot_nt(qbd, ks.astype(BF16)) * d64_ref[...]).astype(BF16)
        o_intra = _dot(p, vb)
        o_ref[pl.ds(r0, SUBLANES), :] = jnp.concatenate(
            [o_intra[h * SUBLANES:(h + 1) * SUBLANES, h * RET_DV:(h + 1) * RET_DV]
             + o_state[h * SUBLANES:(h + 1) * SUBLANES, :] for h in range(RET_HEADS)], axis=1)
        kd = (ks * kdec_ref[...]).astype(BF16)
        for j in range(N_PAIRS):
            upd = _dot_tn(kd[:, j * LANES:(j + 1) * LANES], vb[:, 2 * j * RET_DV:(2 * j + 2) * RET_DV])
            so_ref[s, 2 * j] = s0[2 * j] * (gammas[2 * j] ** SUBLANES) + upd[0:RET_DK, 0:RET_DV]
            so_ref[s, 2 * j + 1] = (s0[2 * j + 1] * (gammas[2 * j + 1] ** SUBLANES)
                                    + upd[RET_DK:2 * RET_DK, RET_DV:2 * RET_DV])
        return carry

    lax.fori_loop(0, bs, per_seq, 0)


def _mixer_sample_kernel(x_ref, o_ref, cst_ref, h0_ref,
                         gmix_ref, win_ref, wconv_ref, bconv_ref, wri_ref, br_ref, bi_ref, lam_ref,
                         wlo_ref, gng_ref, gnb_ref, wro_ref, wo_ref,
                         x1_ref, convst_ref, hst_ref):
    bs, steps = x_ref.shape[0], x_ref.shape[1]
    x = _token_major(x_ref)
    nxb = _rms(x, gmix_ref[...]).astype(BF16)

    lx = _dot_blocks(nxb, win_ref, BLK_LX)
    keep = LRU_CONV - 1
    ext = ([cst_ref[:, j * D_LRU:(j + 1) * D_LRU] for j in range(keep)]
           + [lx[t * bs:(t + 1) * bs] for t in range(steps)])
    xc = jnp.concatenate(
        [bconv_ref[...] + sum(ext[t + j] * wconv_ref[j:j + 1, :] for j in range(LRU_CONV)) for t in range(steps)],
        axis=0)
    for j in range(keep):
        convst_ref[j] = ext[steps + j]
    a, u = _lru_gates(xc, _gate_preacts(xc, wri_ref), br_ref, bi_ref, lam_ref)
    h = h0_ref[...]
    hs = []
    for t in range(steps):
        h = a[t * bs:(t + 1) * bs] * h + u[t * bs:(t + 1) * bs]
        hs.append(h)
    hst_ref[...] = h
    lg = _dot_blocks(nxb, win_ref, BLK_LG)
    ya = _dot_blocks((jnp.concatenate(hs, axis=0) * _gelu(lg)).astype(BF16), wlo_ref)

    rg = _dot_blocks(nxb, win_ref, BLK_RG)
    o = _token_major(o_ref)
    yb = _dot_blocks(_head_norm_gate(o, rg, gng_ref, gnb_ref).astype(BF16), wro_ref)

    x1_ref[...] = _merge_out(x, nxb, ya, yb, win_ref, wo_ref).reshape(steps, bs, D_MODEL)


def _ffn_kernel(x_ref, p_ref, *rest, decode):
    keep = FFN_CONV - 1
    if decode:
        (cst_ref, gffn_ref, wup_ref, wconv_ref, bconv_ref, wdown_ref, wple_ref, gple_ref, wgate_ref, gfin_ref,
         y_ref, convst_ref) = rest
        steps, bs = x_ref.shape[0], x_ref.shape[1]
        x = x_ref[...].reshape(steps * bs, D_MODEL)
        p = _token_major(p_ref)
    else:
        (gffn_ref, wup_ref, wconv_ref, bconv_ref, wdown_ref, wple_ref, gple_ref, wgate_ref, gfin_ref,
         y_ref, convst_ref, z_ref, carry_ref) = rest
        t = pl.program_id(1)

        @pl.when(t == 0)
        def _():
            carry_ref[...] = jnp.zeros_like(carry_ref)

        x = x_ref[...]
        p = p_ref[...]
    rows = x.shape[0]
    nxb = _rms(x, gffn_ref[...]).astype(BF16)
    n_chunks = D_FF // FFN_CHUNK
    act_chunks = []
    for c in range(n_chunks):
        halves = []
        for half in range(2):
            blk = half * n_chunks + c
            cols = slice(blk * FFN_CHUNK, (blk + 1) * FFN_CHUNK)
            up = _dot(nxb, wup_ref[blk][...])
            w = wconv_ref[:, cols]
            bias = bconv_ref[:, cols]
            if decode:
                ext = ([cst_ref[:, j * 2 * D_FF + cols.start:j * 2 * D_FF + cols.stop] for j in range(keep)]
                       + [up[s * bs:(s + 1) * bs] for s in range(steps)])
                halves.append(jnp.concatenate(
                    [bias + sum(ext[s + j] * w[j:j + 1] for j in range(FFN_CONV)) for s in range(steps)], axis=0))
                for j in range(keep):
                    convst_ref[j, :, cols] = ext[steps + j]
            else:
                zc_ref = z_ref.at[2 * (c % 2) + half]
                zc_ref[0:SUBLANES, :] = carry_ref[:, cols]
                zc_ref[SUBLANES:SUBLANES + rows, :] = up
                y = bias + up * w[keep:keep + 1]
                for j in range(keep):
                    start = SUBLANES - keep + j
                    y = y + zc_ref[start:start + rows, :] * w[j:j + 1]
                carry_ref[:, cols] = zc_ref[rows:rows + SUBLANES, :]
                halves.append(y)
        act_chunks.append((_gelu(halves[0]) * halves[1]).astype(BF16))
    x2 = x + _dot_blocks(jnp.concatenate(act_chunks, axis=1), wdown_ref)

    e = _rms(_dot_blocks(p.astype(BF16), wple_ref), gple_ref[...])
    x3 = x2 + _sigmoid(_dot_blocks(x2.astype(BF16), wgate_ref)) * e
    y = _rms(x3, gfin_ref[...])

    if decode:
        for s in range(steps):
            y_ref[:, s, :] = y[s * bs:(s + 1) * bs]
    else:
        y_ref[...] = y

        @pl.when(t == pl.num_programs(1) - 1)
        def _():
            convst_ref[...] = carry_ref[SUBLANES - keep:SUBLANES, :]


def _const_spec(shape):
    n = len(shape)
    return pl.BlockSpec(shape, lambda *_: (0,) * n, pipeline_mode=pl.Buffered(1))


def _row(v):
    return v.reshape(1, -1)


def _rotary_tables(first_pos, n):
    half = RET_DK // 2
    inv = ROPE_BASE ** (-np.arange(half, dtype=np.float64) / half)
    ang = (first_pos + np.arange(n, dtype=np.float64))[:, None] * inv[None, :]
    cos, sin = np.cos(ang), np.sin(ang)
    cos_t = np.tile(cos, (1, LANES // half))
    sin_t = np.tile(np.concatenate([-sin, sin], axis=1), (1, LANES // RET_DK))
    return jnp.asarray(cos_t, F32), jnp.asarray(sin_t, F32)


def _decay_tables(chunk):
    log_gamma = np.log1p(-(2.0 ** (-5.0 - np.arange(RET_HEADS, dtype=np.float64))))
    idx = np.arange(chunk)
    rel = idx[:, None] - idx[None, :]
    dmask = np.where(rel[None] >= 0, np.exp(log_gamma[:, None, None] * np.maximum(rel, 0)[None]), 0.0)
    q_decay = np.exp(log_gamma[None, :] * (idx + 1)[:, None])
    k_decay = np.exp(log_gamma[None, :] * (chunk - 1 - idx)[:, None])
    chunk_decay = np.exp(log_gamma * chunk)
    return dmask, q_decay, k_decay, chunk_decay


def _block_diag_gates(w_r, w_i):
    per = LRU_GROUP // LRU_BLOCK
    eye = jnp.eye(per, dtype=w_r.dtype)

    def pack(w):
        w4 = w.reshape(N_LRU_GROUPS, per, LRU_BLOCK, LRU_BLOCK)
        return jnp.einsum('gbij,bc->gbicj', w4, eye).reshape(N_LRU_GROUPS, LRU_GROUP, LRU_GROUP)

    return jnp.concatenate([pack(w_r), pack(w_i)], axis=-1).astype(BF16)


def _whole(a):
    return ([a], [_const_spec(a.shape)], None)


def _col_blocks(w, first=0, count=None):
    k, n = w.shape
    count = n // W_BLOCK - first if count is None else count
    wb = w.astype(BF16)
    specs = [pl.BlockSpec((k, W_BLOCK), functools.partial(lambda j, *_: (0, j), first + i),
                          pipeline_mode=pl.Buffered(1)) for i in range(count)]
    return ([wb] * count, specs, count)


def _flatten(groups):
    arrays = [a for g in groups for a in g[0]]
    specs = [sp for g in groups for sp in g[1]]
    return arrays, specs, [g[2] for g in groups]


def _grouped(body, counts):
    def kernel_fn(*refs):
        args, i = [], 0
        for n in counts:
            if n is None:
                args.append(refs[i])
                i += 1
            else:
                args.append(list(refs[i:i + n]))
                i += n
        assert i == len(refs)
        return body(*args)
    return kernel_fn


def _one(a, spec):
    return ([a], [spec], None)


def _call(body, name, grid, operands, out_specs, out_shape, scratch_shapes=()):
    arrays, specs, counts = _flatten(operands)
    counts = counts + [None] * (len(out_shape) + len(scratch_shapes))
    return pl.pallas_call(
        _grouped(body, counts), grid=grid, in_specs=specs, out_specs=out_specs, out_shape=out_shape,
        scratch_shapes=list(scratch_shapes),
        compiler_params=pltpu.CompilerParams(dimension_semantics=("arbitrary",) * len(grid),
                                             vmem_limit_bytes=VMEM_LIMIT),
        name=name)(*arrays)


def kernel(x_prompt, x_sample, p_prompt, p_sample, state_lru_conv, state_lru_h, state_ret, state_ffn_conv, g_mix, w_in, w_lru_conv, b_lru_conv, w_r, b_r, w_i, b_i, lru_lambda, w_lru_out, gn_g, gn_b, w_ret_out, w_o, g_ffn, w_up, w_ffn_conv, b_ffn_conv, w_down, w_ple, g_ple, w_ple_gate, g_final):
    depth = g_mix.shape[0]
    assert depth == 1
    bp, lp, _ = x_prompt.shape
    bsq, ls, _ = x_sample.shape
    assert ls == SUBLANES and lp % T_MIX == 0 and lp % T_FFN == 0
    assert bsq % BS_RET == 0 and bsq % BS_MIX == 0
    hdv = RET_HEADS * RET_DV
    hdk = RET_HEADS * RET_DK

    mixer_weights = [_whole(_row(g_mix[0])), _col_blocks(w_in[0]), _whole(w_lru_conv[0]), _whole(_row(b_lru_conv[0])),
                     _whole(_block_diag_gates(w_r[0], w_i[0])), _whole(_row(b_r[0])), _whole(_row(b_i[0])),
                     _whole(_row(lru_lambda[0])), _col_blocks(w_lru_out[0]), _whole(_row(gn_g[0])),
                     _whole(_row(gn_b[0])), _col_blocks(w_ret_out[0]), _col_blocks(w_o[0])]
    ffn_weights = [_whole(_row(g_ffn[0])), _col_blocks(w_up[0]), _whole(w_ffn_conv[0]), _whole(_row(b_ffn_conv[0])),
                   _col_blocks(w_down[0]), _col_blocks(w_ple[0]), _whole(_row(g_ple[0])),
                   _col_blocks(w_ple_gate[0]), _whole(_row(g_final))]

    cos_p, sin_p = _rotary_tables(0, lp)
    dmask, q_decay, k_decay, chunk_decay = _decay_tables(T_MIX)
    dmask = jnp.asarray(dmask, F32)
    qdec_p = jnp.asarray(np.repeat(q_decay, RET_DV, axis=1), F32)
    kdec_p = jnp.asarray(np.repeat(k_decay, RET_DK, axis=1), F32)
    cdec_p = jnp.asarray(np.broadcast_to(np.repeat(chunk_decay, RET_DK).reshape(N_PAIRS, 2 * RET_DK, 1),
                                         (N_PAIRS, 2 * RET_DK, RET_DV)), F32)
    tile = lambda w: pl.BlockSpec((None, T_MIX, w), lambda b, t: (b, t, 0))
    table = pl.BlockSpec((T_MIX, LANES), lambda b, t: (t, 0))
    x1_p, convst_p, hst_p, sst_p = _call(
        _mixer_prompt_kernel, "mixer_prompt", (bp, lp // T_MIX),
        [_one(x_prompt, tile(D_MODEL)), _one(cos_p, table), _one(sin_p, table),
         _whole(dmask), _whole(qdec_p), _whole(kdec_p), _whole(cdec_p)] + mixer_weights,
        out_specs=[tile(D_MODEL),
                   pl.BlockSpec((None, LRU_CONV - 1, D_LRU), lambda b, t: (b, 0, 0)),
                   pl.BlockSpec((None, 1, D_LRU), lambda b, t: (b, 0, 0)),
                   pl.BlockSpec((None, N_PAIRS, 2 * RET_DK, RET_DV), lambda b, t: (b, 0, 0, 0))],
        out_shape=[jax.ShapeDtypeStruct((bp, lp, D_MODEL), F32),
                   jax.ShapeDtypeStruct((bp, LRU_CONV - 1, D_LRU), F32),
                   jax.ShapeDtypeStruct((bp, 1, D_LRU), F32),
                   jax.ShapeDtypeStruct((bp, N_PAIRS, 2 * RET_DK, RET_DV), F32)],
        scratch_shapes=[pltpu.VMEM((T_MIX + 2 * SUBLANES, w_in.shape[2]), F32), pltpu.VMEM((SUBLANES, D_LRU), F32),
                        pltpu.VMEM((N_PAIRS, 2 * RET_DK, RET_DV), F32)])

    tilef = lambda w: pl.BlockSpec((None, T_FFN, w), lambda b, t: (b, t, 0))
    y_p, ffnst_p = _call(
        functools.partial(_ffn_kernel, decode=False), "ffn_prompt", (bp, lp // T_FFN),
        [_one(x1_p, tilef(D_MODEL)), _one(p_prompt[0], tilef(PLE_DIM))] + ffn_weights,
        out_specs=[tilef(D_MODEL), pl.BlockSpec((None, FFN_CONV - 1, 2 * D_FF), lambda b, t: (b, 0, 0))],
        out_shape=[jax.ShapeDtypeStruct((bp, lp, D_MODEL), F32),
                   jax.ShapeDtypeStruct((bp, FFN_CONV - 1, 2 * D_FF), F32)],
        scratch_shapes=[pltpu.VMEM((4, T_FFN + 2 * SUBLANES, FFN_CHUNK), F32),
                        pltpu.VMEM((SUBLANES, 2 * D_FF), F32)])

    rows_s = bsq * ls
    cos_s, sin_s = _rotary_tables(PAST_LEN, ls)
    dmask8, q_decay8, k_decay8, _ = _decay_tables(ls)
    d64 = jnp.asarray(dmask8.reshape(RET_HEADS * ls, ls), F32)
    qdec64 = jnp.asarray(np.broadcast_to(q_decay8.T.reshape(RET_HEADS * ls, 1), (RET_HEADS * ls, RET_DV)), F32)
    kdec8 = jnp.asarray(np.repeat(k_decay8, RET_DK, axis=1), F32)
    rr = BS_RET * ls
    state_block = pl.BlockSpec((BS_RET, RET_HEADS, RET_DK, RET_DV), lambda i: (i, 0, 0, 0))
    o_s, sst_s = _call(
        _ret_sample_kernel, "ret_sample", (bsq // BS_RET,),
        [_one(x_sample.reshape(rows_s, D_MODEL), pl.BlockSpec((rr, D_MODEL), lambda i: (i, 0))),
         _one(state_ret[0], state_block),
         _whole(cos_s), _whole(sin_s), _whole(d64), _whole(qdec64), _whole(kdec8), _whole(_row(g_mix[0])),
         _col_blocks(w_in[0], BLK_Q[0], BLK_RG[0] - BLK_Q[0])],
        out_specs=[pl.BlockSpec((rr, hdv), lambda i: (i, 0)), state_block],
        out_shape=[jax.ShapeDtypeStruct((rows_s, hdv), F32),
                   jax.ShapeDtypeStruct((bsq, RET_HEADS, RET_DK, RET_DV), F32)],
        scratch_shapes=[pltpu.VMEM((rr, hdk), F32), pltpu.VMEM((rr, hdk), F32), pltpu.VMEM((rr, hdv), F32)])

    seqs = lambda *dims: pl.BlockSpec((BS_MIX,) + dims, lambda i: (i,) + (0,) * len(dims))
    by_row = lambda rows, c: pl.BlockSpec((rows, BS_MIX, c), lambda i: (0, i, 0))
    keep_l, keep_f = LRU_CONV - 1, FFN_CONV - 1
    x1_s, convst_s, hst_s = _call(
        _mixer_sample_kernel, "mixer_sample", (bsq // BS_MIX,),
        [_one(x_sample, seqs(ls, D_MODEL)), _one(o_s.reshape(bsq, ls, hdv), seqs(ls, hdv)),
         _one(state_lru_conv[0].reshape(bsq, keep_l * D_LRU), seqs(keep_l * D_LRU)),
         _one(state_lru_h[0], seqs(D_LRU))] + mixer_weights,
        out_specs=[by_row(ls, D_MODEL), by_row(keep_l, D_LRU), seqs(D_LRU)],
        out_shape=[jax.ShapeDtypeStruct((ls, bsq, D_MODEL), F32),
                   jax.ShapeDtypeStruct((keep_l, bsq, D_LRU), F32),
                   jax.ShapeDtypeStruct((bsq, D_LRU), F32)])

    y_s, ffnst_s = _call(
        functools.partial(_ffn_kernel, decode=True), "ffn_sample", (bsq // BS_MIX,),
        [_one(x1_s, by_row(ls, D_MODEL)), _one(p_sample[0], seqs(ls, PLE_DIM)),
         _one(state_ffn_conv[0].reshape(bsq, keep_f * 2 * D_FF), seqs(keep_f * 2 * D_FF))] + ffn_weights,
        out_specs=[seqs(ls, D_MODEL), by_row(keep_f, 2 * D_FF)],
        out_shape=[jax.ShapeDtypeStruct((bsq, ls, D_MODEL), F32),
                   jax.ShapeDtypeStruct((keep_f, bsq, 2 * D_FF), F32)])

    return (y_p,
            y_s,
            convst_p[None],
            hst_p.reshape(1, bp, D_LRU),
            sst_p.reshape(1, bp, RET_HEADS, RET_DK, RET_DV),
            ffnst_p[None],
            jnp.swapaxes(convst_s, 0, 1)[None],
            hst_s[None],
            sst_s[None],
            jnp.swapaxes(ffnst_s, 0, 1)[None])
```

```python
import functools
import math

import jax
import jax.numpy as jnp
import numpy as np
from jax import lax
from jax.experimental import pallas as pl
from jax.experimental.pallas import tpu as pltpu

F32 = jnp.float32
BF16 = jnp.bfloat16

D_MODEL = 1024
PLE_DIM = 256
D_LRU = 1024
LRU_BLOCK = 64
LRU_GROUP = 256
N_LRU_GROUPS = D_LRU // LRU_GROUP
LRU_CONV = 4
LRU_C = 8.0
RET_HEADS = 8
RET_DK = 64
RET_DV = 128
N_PAIRS = RET_HEADS // 2
ROPE_BASE = 10000.0
D_FF = 3072
FFN_CONV = 3
FFN_CHUNK = 512
DOWN_GROUP = 2
EPS = 1e-6
PAST_LEN = 16384
SUBLANES = 8
LANES = 128

W_BLOCK = 512
BLK_LX, BLK_LG, BLK_Q, BLK_K, BLK_V, BLK_RG, BLK_GA, BLK_GB = (0, 2), (2, 2), (4, 1), (5, 1), (6, 2), (8, 2), (10, 2), (12, 2)

T_MIX = 256
T_FFN = 512
BS_RET = 32
BS_MIX = 64
VMEM_LIMIT = 56 * 1024 * 1024


def _sigmoid(x):
    return 1.0 / (1.0 + jnp.exp(-x))


def _gelu(x):
    c = math.sqrt(2.0 / math.pi)
    return x * (0.5 + 0.5 * jnp.tanh(x * (c + (c * 0.044715) * (x * x))))


def _rms(x, g):
    ms = jnp.mean(x * x, axis=-1, keepdims=True)
    return x * lax.rsqrt(ms + EPS) * g


def _dot(a, b):
    return jnp.dot(a, b, preferred_element_type=F32)


def _dot_blocks(a, w_refs, blocks=None):
    first, count = blocks if blocks is not None else (0, len(w_refs))
    parts = [_dot(a, w_refs[first + i][...]) for i in range(count)]
    return parts[0] if count == 1 else jnp.concatenate(parts, axis=1)


def _dot_nt(a, b):
    return lax.dot_general(a, b, (((1,), (1,)), ((), ())), preferred_element_type=F32)


def _dot_tn(a, b):
    return lax.dot_general(a, b, (((0,), (0,)), ((), ())), preferred_element_type=F32)


def _rotary_slab(x, cos, sin_signed):
    lane = lax.broadcasted_iota(jnp.int32, x.shape, x.ndim - 1)
    first_half = (lane & (RET_DK - 1)) < (RET_DK // 2)
    other = jnp.where(first_half,
                      pltpu.roll(x, LANES - RET_DK // 2, axis=x.ndim - 1),
                      pltpu.roll(x, RET_DK // 2, axis=x.ndim - 1))
    return x * cos + other * sin_signed


def _token_major(ref):
    return jnp.concatenate([ref[:, t, :] for t in range(ref.shape[1])], axis=0)


def _gate_preacts(xc, wri_ref):
    xcb = xc.astype(BF16)
    return [_dot(xcb[:, g * LRU_GROUP:(g + 1) * LRU_GROUP], wri_ref[g]) for g in range(N_LRU_GROUPS)]


def _lru_gates(xc, pre, br_ref, bi_ref, lam_ref):
    pre_r = [p[:, :LRU_GROUP] for p in pre]
    pre_i = [p[:, LRU_GROUP:] for p in pre]
    r = _sigmoid(jnp.concatenate(pre_r, axis=1) + br_ref[...])
    ig = _sigmoid(jnp.concatenate(pre_i, axis=1) + bi_ref[...])
    lam = lam_ref[...]
    softplus_neg_lam = jnp.maximum(-lam, 0.0) + jnp.log1p(jnp.exp(-jnp.abs(lam)))
    log_a = (-LRU_C * r) * softplus_neg_lam
    a = jnp.exp(log_a)
    u = jnp.sqrt(1.0 - a * a) * (ig * xc)
    return a, u


def _scan_within_groups(a3, u3):
    row = lax.broadcasted_iota(jnp.int32, (1,) + a3.shape[1:], 1)
    for s in (1, 2, 4):
        valid = row >= s
        a_prev = jnp.where(valid, pltpu.roll(a3, s, axis=1), 1.0)
        u_prev = jnp.where(valid, pltpu.roll(u3, s, axis=1), 0.0)
        u3 = a3 * u_prev + u3
        a3 = a3 * a_prev
    return a3, u3


def _head_norm_gate(o, rg, gng_ref, gnb_ref):
    parts = []
    for h in range(RET_HEADS):
        oh = o[:, h * RET_DV:(h + 1) * RET_DV]
        mu = jnp.mean(oh, axis=-1, keepdims=True)
        ctr = oh - mu
        var = jnp.mean(ctr * ctr, axis=-1, keepdims=True)
        parts.append(ctr * lax.rsqrt(var + EPS))
    y = jnp.concatenate(parts, axis=1) * gng_ref[...] + gnb_ref[...]
    return y * (rg * _sigmoid(rg))


def _merge_out(x, nxb, ya, yb, win_ref, wo_ref):
    ga = _dot_blocks(nxb, win_ref, BLK_GA)
    gb = _dot_blocks(nxb, win_ref, BLK_GB)
    merged = _sigmoid(ga) * ya + _sigmoid(gb) * yb
    return x + _dot_blocks(merged.astype(BF16), wo_ref)


def _mixer_prompt_kernel(x_ref, cos_ref, sin_ref, dmask_ref, qdec_ref, kdec_ref, cdec_ref,
                         gmix_ref, win_ref, wconv_ref, bconv_ref, wri_ref, br_ref, bi_ref, lam_ref,
                         wlo_ref, gng_ref, gnb_ref, wro_ref, wo_ref,
                         x1_ref, convst_ref, hst_ref, sst_ref,
                         proj_ref, hc_ref, s_ref):
    T = T_MIX
    t = pl.program_id(1)

    @pl.when(t == 0)
    def _():
        proj_ref[0:SUBLANES, :] = jnp.zeros((SUBLANES, proj_ref.shape[1]), F32)
        hc_ref[...] = jnp.zeros_like(hc_ref)
        s_ref[...] = jnp.zeros_like(s_ref)

    x = x_ref[...]
    nxb = _rms(x, gmix_ref[...]).astype(BF16)

    def project(first, count):
        for blk in range(first, first + count):
            proj_ref[SUBLANES:SUBLANES + T, blk * W_BLOCK:(blk + 1) * W_BLOCK] = _dot(nxb, win_ref[blk][...])

    project(0, BLK_Q[0])

    def proj(blocks):
        first, count = blocks
        return proj_ref[SUBLANES:SUBLANES + T, first * W_BLOCK:(first + count) * W_BLOCK]

    lx_cols = slice(BLK_LX[0] * W_BLOCK, (BLK_LX[0] + BLK_LX[1]) * W_BLOCK)
    xc = bconv_ref[...] + proj(BLK_LX) * wconv_ref[LRU_CONV - 1:LRU_CONV, :]
    for j in range(LRU_CONV - 1):
        start = SUBLANES - (LRU_CONV - 1) + j
        xc = xc + proj_ref[start:start + T, lx_cols] * wconv_ref[j:j + 1, :]
    pre = _gate_preacts(xc, wri_ref)
    project(BLK_Q[0], len(win_ref) - BLK_Q[0])
    a, u = _lru_gates(xc, pre, br_ref, bi_ref, lam_ref)
    G = T // SUBLANES
    A3, B3 = _scan_within_groups(a.reshape(G, SUBLANES, D_LRU), u.reshape(G, SUBLANES, D_LRU))
    h_in = hc_ref[...]
    hs_groups = []
    for g in range(G):
        h = A3[g] * h_in + B3[g]
        hs_groups.append(h)
        h_in = jnp.broadcast_to(h[SUBLANES - 1:SUBLANES, :], (SUBLANES, D_LRU))
    h_last = h_in
    hc_ref[...] = h_last
    hs = jnp.concatenate(hs_groups, axis=0)
    ya = _dot_blocks((hs * _gelu(proj(BLK_LG))).astype(BF16), wlo_ref)

    q = proj(BLK_Q)
    k = proj(BLK_K)
    v = proj(BLK_V)
    cos = cos_ref[...]
    sin_signed = sin_ref[...]
    lane = lax.broadcasted_iota(jnp.int32, (T, LANES), 1)
    o_parts = []
    for j in range(N_PAIRS):
        sl = slice(j * LANES, (j + 1) * LANES)
        qr = _rotary_slab(q[:, sl], cos, sin_signed)
        kr = _rotary_slab(k[:, sl], cos, sin_signed) * (RET_DK ** -0.5)
        kb = kr.astype(BF16)
        s_pair = s_ref[j]
        s_pair_b = s_pair.astype(BF16)
        for half in range(2):
            h = 2 * j + half
            qm = jnp.where((lane >> 6) == half, qr, 0.0).astype(BF16)
            p = (_dot_nt(qm, kb) * dmask_ref[h]).astype(BF16)
            vb = v[:, h * RET_DV:(h + 1) * RET_DV].astype(BF16)
            o_parts.append(_dot(p, vb) + _dot(qm, s_pair_b) * qdec_ref[:, h * RET_DV:(h + 1) * RET_DV])
        kd = (kr * kdec_ref[:, sl]).astype(BF16)
        upd = _dot_tn(kd, v[:, 2 * j * RET_DV:(2 * j + 2) * RET_DV].astype(BF16))
        s_ref[j] = s_pair * cdec_ref[j] + jnp.concatenate(
            [upd[0:RET_DK, 0:RET_DV], upd[RET_DK:2 * RET_DK, RET_DV:2 * RET_DV]], axis=0)
    o = jnp.concatenate(o_parts, axis=1)
    yb = _dot_blocks(_head_norm_gate(o, proj(BLK_RG), gng_ref, gnb_ref).astype(BF16), wro_ref)

    merged = _sigmoid(proj(BLK_GA)) * ya + _sigmoid(proj(BLK_GB)) * yb
    x1_ref[...] = x + _dot_blocks(merged.astype(BF16), wo_ref)

    proj_ref[0:SUBLANES, lx_cols] = proj_ref[T:T + SUBLANES, lx_cols]

    @pl.when(t == pl.num_programs(1) - 1)
    def _():
        convst_ref[...] = proj_ref[SUBLANES - (LRU_CONV - 1):SUBLANES, lx_cols]
        hst_ref[...] = h_last[0:1, :]
        sst_ref[...] = s_ref[...]


def _ret_sample_kernel(x_ref, st_ref, cos_ref, sin_ref, d64_ref, qdec_ref, kdec_ref,
                       gmix_ref, wqkv_ref,
                       o_ref, so_ref,
                       q_ref, k_ref, v_ref):
    bs = BS_RET
    nq = RET_HEADS * RET_DK
    nxb = _rms(x_ref[...], gmix_ref[...]).astype(BF16)
    qkv = _dot_blocks(nxb, wqkv_ref)
    cos = cos_ref[...][None]
    sin_signed = sin_ref[...][None]
    for j in range(N_PAIRS):
        sl = slice(j * LANES, (j + 1) * LANES)
        q3 = qkv[:, j * LANES:(j + 1) * LANES].reshape(bs, SUBLANES, LANES)
        k3 = qkv[:, nq + j * LANES:nq + (j + 1) * LANES].reshape(bs, SUBLANES, LANES)
        q_ref[:, sl] = _rotary_slab(q3, cos, sin_signed).reshape(bs * SUBLANES, LANES)
        k_ref[:, sl] = (_rotary_slab(k3, cos, sin_signed) * (RET_DK ** -0.5)).reshape(bs * SUBLANES, LANES)
    v_ref[...] = qkv[:, 2 * nq:]

    lane = lax.broadcasted_iota(jnp.int32, (SUBLANES, nq), 1)
    gammas = [1.0 - 2.0 ** (-5.0 - h) for h in range(RET_HEADS)]

    def per_seq(s, carry):
        r0 = pl.multiple_of(s * SUBLANES, SUBLANES)
        qs = q_ref[pl.ds(r0, SUBLANES), :]
        ks = k_ref[pl.ds(r0, SUBLANES), :]
        vs = v_ref[pl.ds(r0, SUBLANES), :]
        vb = vs.astype(BF16)
        s0 = st_ref[s]
        qbd = jnp.concatenate([jnp.where((lane >> 6) == h, qs, 0.0) for h in range(RET_HEADS)],
                              axis=0).astype(BF16)
        o_state = _dot(qbd, s0.reshape(RET_HEADS * RET_DK, RET_DV).astype(BF16)) * qdec_ref[...]
        p = (_dot_nt(qbd, ks.astype(BF16)) * d64_ref[...]).astype(BF16)
        o_intra = _dot(p, vb)
        o_ref[pl.ds(r0, SUBLANES), :] = jnp.concatenate(
            [o_intra[h * SUBLANES:(h + 1) * SUBLANES, h * RET_DV:(h + 1) * RET_DV]
             + o_state[h * SUBLANES:(h + 1) * SUBLANES, :] for h in range(RET_HEADS)], axis=1)
        kd = (ks * kdec_ref[...]).astype(BF16)
        for j in range(N_PAIRS):
            upd = _dot_tn(kd[:, j * LANES:(j + 1) * LANES], vb[:, 2 * j * RET_DV:(2 * j + 2) * RET_DV])
            so_ref[s, 2 * j] = s0[2 * j] * (gammas[2 * j] ** SUBLANES) + upd[0:RET_DK, 0:RET_DV]
            so_ref[s, 2 * j + 1] = (s0[2 * j + 1] * (gammas[2 * j + 1] ** SUBLANES)
                                    + upd[RET_DK:2 * RET_DK, RET_DV:2 * RET_DV])
        return carry

    lax.fori_loop(0, bs, per_seq, 0, unroll=4)


def _mixer_sample_kernel(x_ref, o_ref, cst_ref, h0_ref,
                         gmix_ref, win_ref, wconv_ref, bconv_ref, wri_ref, br_ref, bi_ref, lam_ref,
                         wlo_ref, gng_ref, gnb_ref, wro_ref, wo_ref,
                         x1_ref, convst_ref, hst_ref):
    bs, steps = x_ref.shape[0], x_ref.shape[1]
    x = _token_major(x_ref)
    nxb = _rms(x, gmix_ref[...]).astype(BF16)

    lx = _dot_blocks(nxb, win_ref, BLK_LX)
    keep = LRU_CONV - 1
    ext = ([cst_ref[:, j * D_LRU:(j + 1) * D_LRU] for j in range(keep)]
           + [lx[t * bs:(t + 1) * bs] for t in range(steps)])
    xc = jnp.concatenate(
        [bconv_ref[...] + sum(ext[t + j] * wconv_ref[j:j + 1, :] for j in range(LRU_CONV)) for t in range(steps)],
        axis=0)
    for j in range(keep):
        convst_ref[j] = ext[steps + j]
    a, u = _lru_gates(xc, _gate_preacts(xc, wri_ref), br_ref, bi_ref, lam_ref)
    h = h0_ref[...]
    hs = []
    for t in range(steps):
        h = a[t * bs:(t + 1) * bs] * h + u[t * bs:(t + 1) * bs]
        hs.append(h)
    hst_ref[...] = h
    lg = _dot_blocks(nxb, win_ref, BLK_LG)
    ya = _dot_blocks((jnp.concatenate(hs, axis=0) * _gelu(lg)).astype(BF16), wlo_ref)

    rg = _dot_blocks(nxb, win_ref, BLK_RG)
    o = _token_major(o_ref)
    yb = _dot_blocks(_head_norm_gate(o, rg, gng_ref, gnb_ref).astype(BF16), wro_ref)

    x1_ref[...] = _merge_out(x, nxb, ya, yb, win_ref, wo_ref).reshape(steps, bs, D_MODEL)


def _ffn_kernel(x_ref, p_ref, *rest, decode):
    keep = FFN_CONV - 1
    if decode:
        (cst_ref, gffn_ref, wup_ref, wconv_ref, bconv_ref, wdown_ref, wple_ref, gple_ref, wgate_ref, gfin_ref,
         y_ref, convst_ref) = rest
        steps, bs = x_ref.shape[0], x_ref.shape[1]
        x = x_ref[...].reshape(steps * bs, D_MODEL)
        p = _token_major(p_ref)
    else:
        (gffn_ref, wup_ref, wconv_ref, bconv_ref, wdown_ref, wple_ref, gple_ref, wgate_ref, gfin_ref,
         y_ref, convst_ref, z_ref, carry_ref) = rest
        t = pl.program_id(1)

        @pl.when(t == 0)
        def _():
            carry_ref[...] = jnp.zeros_like(carry_ref)

        x = x_ref[...]
        p = p_ref[...]
    rows = x.shape[0]
    nxb = _rms(x, gffn_ref[...]).astype(BF16)
    n_chunks = D_FF // FFN_CHUNK
    act_chunks = []
    acc = None
    for c in range(n_chunks):
        halves = []
        for half in range(2):
            blk = half * n_chunks + c
            cols = slice(blk * FFN_CHUNK, (blk + 1) * FFN_CHUNK)
            up = _dot(nxb, wup_ref[blk][...])
            w = wconv_ref[:, cols]
            bias = bconv_ref[:, cols]
            if decode:
                ext = ([cst_ref[:, j * 2 * D_FF + cols.start:j * 2 * D_FF + cols.stop] for j in range(keep)]
                       + [up[s * bs:(s + 1) * bs] for s in range(steps)])
                halves.append(jnp.concatenate(
                    [bias + sum(ext[s + j] * w[j:j + 1] for j in range(FFN_CONV)) for s in range(steps)], axis=0))
                for j in range(keep):
                    convst_ref[j, :, cols] = ext[steps + j]
            else:
                zc_ref = z_ref.at[2 * (c % 2) + half]
                zc_ref[0:SUBLANES, :] = carry_ref[:, cols]
                zc_ref[SUBLANES:SUBLANES + rows, :] = up
                y = bias + up * w[keep:keep + 1]
                for j in range(keep):
                    start = SUBLANES - keep + j
                    y = y + zc_ref[start:start + rows, :] * w[j:j + 1]
                carry_ref[:, cols] = zc_ref[rows:rows + SUBLANES, :]
                halves.append(y)
        act_chunks.append((_gelu(halves[0]) * halves[1]).astype(BF16))
        if len(act_chunks) == DOWN_GROUP:
            k0 = (c + 1 - DOWN_GROUP) * FFN_CHUNK
            part = jnp.concatenate(
                [_dot(jnp.concatenate(act_chunks, axis=1), w[k0:k0 + DOWN_GROUP * FFN_CHUNK, :]) for w in wdown_ref],
                axis=1)
            acc = part if acc is None else acc + part
            act_chunks = []
    x2 = x + acc

    e = _rms(_dot_blocks(p.astype(BF16), wple_ref), gple_ref[...])
    x3 = x2 + _sigmoid(_dot_blocks(x2.astype(BF16), wgate_ref)) * e
    y = _rms(x3, gfin_ref[...])

    if decode:
        for s in range(steps):
            y_ref[:, s, :] = y[s * bs:(s + 1) * bs]
    else:
        y_ref[...] = y

        @pl.when(t == pl.num_programs(1) - 1)
        def _():
            convst_ref[...] = carry_ref[SUBLANES - keep:SUBLANES, :]


def _const_spec(shape):
    n = len(shape)
    return pl.BlockSpec(shape, lambda *_: (0,) * n, pipeline_mode=pl.Buffered(1))


def _row(v):
    return v.reshape(1, -1)


def _rotary_tables(first_pos, n):
    half = RET_DK // 2
    inv = ROPE_BASE ** (-np.arange(half, dtype=np.float64) / half)
    ang = (first_pos + np.arange(n, dtype=np.float64))[:, None] * inv[None, :]
    cos, sin = np.cos(ang), np.sin(ang)
    cos_t = np.tile(cos, (1, LANES // half))
    sin_t = np.tile(np.concatenate([-sin, sin], axis=1), (1, LANES // RET_DK))
    return jnp.asarray(cos_t, F32), jnp.asarray(sin_t, F32)


def _decay_tables(chunk):
    log_gamma = np.log1p(-(2.0 ** (-5.0 - np.arange(RET_HEADS, dtype=np.float64))))
    idx = np.arange(chunk)
    rel = idx[:, None] - idx[None, :]
    dmask = np.where(rel[None] >= 0, np.exp(log_gamma[:, None, None] * np.maximum(rel, 0)[None]), 0.0)
    q_decay = np.exp(log_gamma[None, :] * (idx + 1)[:, None])
    k_decay = np.exp(log_gamma[None, :] * (chunk - 1 - idx)[:, None])
    chunk_decay = np.exp(log_gamma * chunk)
    return dmask, q_decay, k_decay, chunk_decay


def _block_diag_gates(w_r, w_i):
    per = LRU_GROUP // LRU_BLOCK
    eye = jnp.eye(per, dtype=w_r.dtype)

    def pack(w):
        w4 = w.reshape(N_LRU_GROUPS, per, LRU_BLOCK, LRU_BLOCK)
        return jnp.einsum('gbij,bc->gbicj', w4, eye).reshape(N_LRU_GROUPS, LRU_GROUP, LRU_GROUP)

    return jnp.concatenate([pack(w_r), pack(w_i)], axis=-1).astype(BF16)


def _whole(a):
    return ([a], [_const_spec(a.shape)], None)


def _col_blocks(w, first=0, count=None):
    k, n = w.shape
    count = n // W_BLOCK - first if count is None else count
    wb = w.astype(BF16)
    specs = [pl.BlockSpec((k, W_BLOCK), functools.partial(lambda j, *_: (0, j), first + i),
                          pipeline_mode=pl.Buffered(1)) for i in range(count)]
    return ([wb] * count, specs, count)


def _flatten(groups):
    arrays = [a for g in groups for a in g[0]]
    specs = [sp for g in groups for sp in g[1]]
    return arrays, specs, [g[2] for g in groups]


def _grouped(body, counts):
    def kernel_fn(*refs):
        args, i = [], 0
        for n in counts:
            if n is None:
                args.append(refs[i])
                i += 1
            else:
                args.append(list(refs[i:i + n]))
                i += n
        assert i == len(refs)
        return body(*args)
    return kernel_fn


def _one(a, spec):
    return ([a], [spec], None)


def _call(body, name, grid, operands, out_specs, out_shape, scratch_shapes=()):
    arrays, specs, counts = _flatten(operands)
    counts = counts + [None] * (len(out_shape) + len(scratch_shapes))
    return pl.pallas_call(
        _grouped(body, counts), grid=grid, in_specs=specs, out_specs=out_specs, out_shape=out_shape,
        scratch_shapes=list(scratch_shapes),
        compiler_params=pltpu.CompilerParams(dimension_semantics=("arbitrary",) * len(grid),
                                             vmem_limit_bytes=VMEM_LIMIT),
        name=name)(*arrays)


def kernel(x_prompt, x_sample, p_prompt, p_sample, state_lru_conv, state_lru_h, state_ret, state_ffn_conv, g_mix, w_in, w_lru_conv, b_lru_conv, w_r, b_r, w_i, b_i, lru_lambda, w_lru_out, gn_g, gn_b, w_ret_out, w_o, g_ffn, w_up, w_ffn_conv, b_ffn_conv, w_down, w_ple, g_ple, w_ple_gate, g_final):
    depth = g_mix.shape[0]
    assert depth == 1
    bp, lp, _ = x_prompt.shape
    bsq, ls, _ = x_sample.shape
    assert ls == SUBLANES and lp % T_MIX == 0 and lp % T_FFN == 0
    assert bsq % BS_RET == 0 and bsq % BS_MIX == 0
    hdv = RET_HEADS * RET_DV
    hdk = RET_HEADS * RET_DK

    mixer_weights = [_whole(_row(g_mix[0])), _col_blocks(w_in[0]), _whole(w_lru_conv[0]), _whole(_row(b_lru_conv[0])),
                     _whole(_block_diag_gates(w_r[0], w_i[0])), _whole(_row(b_r[0])), _whole(_row(b_i[0])),
                     _whole(_row(lru_lambda[0])), _col_blocks(w_lru_out[0]), _whole(_row(gn_g[0])),
                     _whole(_row(gn_b[0])), _col_blocks(w_ret_out[0]), _col_blocks(w_o[0])]
    ffn_weights = [_whole(_row(g_ffn[0])), _col_blocks(w_up[0]), _whole(w_ffn_conv[0]), _whole(_row(b_ffn_conv[0])),
                   _col_blocks(w_down[0]), _col_blocks(w_ple[0]), _whole(_row(g_ple[0])),
                   _col_blocks(w_ple_gate[0]), _whole(_row(g_final))]

    cos_p, sin_p = _rotary_tables(0, lp)
    dmask, q_decay, k_decay, chunk_decay = _decay_tables(T_MIX)
    dmask = jnp.asarray(dmask, F32)
    qdec_p = jnp.asarray(np.repeat(q_decay, RET_DV, axis=1), F32)
    kdec_p = jnp.asarray(np.repeat(k_decay, RET_DK, axis=1), F32)
    cdec_p = jnp.asarray(np.broadcast_to(np.repeat(chunk_decay, RET_DK).reshape(N_PAIRS, 2 * RET_DK, 1),
                                         (N_PAIRS, 2 * RET_DK, RET_DV)), F32)
    tile = lambda w: pl.BlockSpec((None, T_MIX, w), lambda b, t: (b, t, 0))
    table = pl.BlockSpec((T_MIX, LANES), lambda b, t: (t, 0))
    x1_p, convst_p, hst_p, sst_p = _call(
        _mixer_prompt_kernel, "mixer_prompt", (bp, lp // T_MIX),
        [_one(x_prompt, tile(D_MODEL)), _one(cos_p, table), _one(sin_p, table),
         _whole(dmask), _whole(qdec_p), _whole(kdec_p), _whole(cdec_p)] + mixer_weights,
        out_specs=[tile(D_MODEL),
                   pl.BlockSpec((None, LRU_CONV - 1, D_LRU), lambda b, t: (b, 0, 0)),
                   pl.BlockSpec((None, 1, D_LRU), lambda b, t: (b, 0, 0)),
                   pl.BlockSpec((None, N_PAIRS, 2 * RET_DK, RET_DV), lambda b, t: (b, 0, 0, 0))],
        out_shape=[jax.ShapeDtypeStruct((bp, lp, D_MODEL), F32),
                   jax.ShapeDtypeStruct((bp, LRU_CONV - 1, D_LRU), F32),
                   jax.ShapeDtypeStruct((bp, 1, D_LRU), F32),
                   jax.ShapeDtypeStruct((bp, N_PAIRS, 2 * RET_DK, RET_DV), F32)],
        scratch_shapes=[pltpu.VMEM((T_MIX + 2 * SUBLANES, w_in.shape[2]), F32), pltpu.VMEM((SUBLANES, D_LRU), F32),
                        pltpu.VMEM((N_PAIRS, 2 * RET_DK, RET_DV), F32)])

    tilef = lambda w: pl.BlockSpec((None, T_FFN, w), lambda b, t: (b, t, 0))
    y_p, ffnst_p = _call(
        functools.partial(_ffn_kernel, decode=False), "ffn_prompt", (bp, lp // T_FFN),
        [_one(x1_p, tilef(D_MODEL)), _one(p_prompt[0], tilef(PLE_DIM))] + ffn_weights,
        out_specs=[tilef(D_MODEL), pl.BlockSpec((None, FFN_CONV - 1, 2 * D_FF), lambda b, t: (b, 0, 0))],
        out_shape=[jax.ShapeDtypeStruct((bp, lp, D_MODEL), F32),
                   jax.ShapeDtypeStruct((bp, FFN_CONV - 1, 2 * D_FF), F32)],
        scratch_shapes=[pltpu.VMEM((4, T_FFN + 2 * SUBLANES, FFN_CHUNK), F32),
                        pltpu.VMEM((SUBLANES, 2 * D_FF), F32)])

    rows_s = bsq * ls
    cos_s, sin_s = _rotary_tables(PAST_LEN, ls)
    dmask8, q_decay8, k_decay8, _ = _decay_tables(ls)
    d64 = jnp.asarray(dmask8.reshape(RET_HEADS * ls, ls), F32)
    qdec64 = jnp.asarray(np.broadcast_to(q_decay8.T.reshape(RET_HEADS * ls, 1), (RET_HEADS * ls, RET_DV)), F32)
    kdec8 = jnp.asarray(np.repeat(k_decay8, RET_DK, axis=1), F32)
    rr = BS_RET * ls
    state_block = pl.BlockSpec((BS_RET, RET_HEADS, RET_DK, RET_DV), lambda i: (i, 0, 0, 0))
    o_s, sst_s = _call(
        _ret_sample_kernel, "ret_sample", (bsq // BS_RET,),
        [_one(x_sample.reshape(rows_s, D_MODEL), pl.BlockSpec((rr, D_MODEL), lambda i: (i, 0))),
         _one(state_ret[0], state_block),
         _whole(cos_s), _whole(sin_s), _whole(d64), _whole(qdec64), _whole(kdec8), _whole(_row(g_mix[0])),
         _col_blocks(w_in[0], BLK_Q[0], BLK_RG[0] - BLK_Q[0])],
        out_specs=[pl.BlockSpec((rr, hdv), lambda i: (i, 0)), state_block],
        out_shape=[jax.ShapeDtypeStruct((rows_s, hdv), F32),
                   jax.ShapeDtypeStruct((bsq, RET_HEADS, RET_DK, RET_DV), F32)],
        scratch_shapes=[pltpu.VMEM((rr, hdk), F32), pltpu.VMEM((rr, hdk), F32), pltpu.VMEM((rr, hdv), F32)])

    seqs = lambda *dims: pl.BlockSpec((BS_MIX,) + dims, lambda i: (i,) + (0,) * len(dims))
    by_row = lambda rows, c: pl.BlockSpec((rows, BS_MIX, c), lambda i: (0, i, 0))
    keep_l, keep_f = LRU_CONV - 1, FFN_CONV - 1
    x1_s, convst_s, hst_s = _call(
        _mixer_sample_kernel, "mixer_sample", (bsq // BS_MIX,),
        [_one(x_sample, seqs(ls, D_MODEL)), _one(o_s.reshape(bsq, ls, hdv), seqs(ls, hdv)),
         _one(state_lru_conv[0].reshape(bsq, keep_l * D_LRU), seqs(keep_l * D_LRU)),
         _one(state_lru_h[0], seqs(D_LRU))] + mixer_weights,
        out_specs=[by_row(ls, D_MODEL), by_row(keep_l, D_LRU), seqs(D_LRU)],
        out_shape=[jax.ShapeDtypeStruct((ls, bsq, D_MODEL), F32),
                   jax.ShapeDtypeStruct((keep_l, bsq, D_LRU), F32),
                   jax.ShapeDtypeStruct((bsq, D_LRU), F32)])

    y_s, ffnst_s = _call(
        functools.partial(_ffn_kernel, decode=True), "ffn_sample", (bsq // BS_MIX,),
        [_one(x1_s, by_row(ls, D_MODEL)), _one(p_sample[0], seqs(ls, PLE_DIM)),
         _one(state_ffn_conv[0].reshape(bsq, keep_f * 2 * D_FF), seqs(keep_f * 2 * D_FF))] + ffn_weights,
        out_specs=[seqs(ls, D_MODEL), by_row(keep_f, 2 * D_FF)],
        out_shape=[jax.ShapeDtypeStruct((bsq, ls, D_MODEL), F32),
                   jax.ShapeDtypeStruct((keep_f, bsq, 2 * D_FF), F32)])

    return (y_p,
            y_s,
            convst_p[None],
            hst_p.reshape(1, bp, D_LRU),
            sst_p.reshape(1, bp, RET_HEADS, RET_DK, RET_DV),
            ffnst_p[None],
            jnp.swapaxes(convst_s, 0, 1)[None],
            hst_s[None],
            sst_s[None],
            jnp.swapaxes(ffnst_s, 0, 1)[None])
```

```python
import functools
import math

import jax
import jax.numpy as jnp
import numpy as np
from jax import lax
from jax.experimental import pallas as pl
from jax.experimental.pallas import tpu as pltpu

F32 = jnp.float32
BF16 = jnp.bfloat16

D_MODEL = 1024
PLE_DIM = 256
D_LRU = 1024
LRU_BLOCK = 64
LRU_GROUP = 256
N_LRU_GROUPS = D_LRU // LRU_GROUP
LRU_CONV = 4
LRU_C = 8.0
RET_HEADS = 8
RET_DK = 64
RET_DV = 128
N_PAIRS = RET_HEADS // 2
ROPE_BASE = 10000.0
D_FF = 3072
FFN_CONV = 3
FFN_CHUNK = 512
DOWN_GROUP = 2
EPS = 1e-6
PAST_LEN = 16384
SUBLANES = 8
BF16_ROWS = 16
LANES = 128

W_BLOCK = 512
BLK_LX, BLK_LG, BLK_Q, BLK_K, BLK_V, BLK_RG, BLK_GA, BLK_GB = (0, 2), (2, 2), (4, 1), (5, 1), (6, 2), (8, 2), (10, 2), (12, 2)

T_MIX = 256
T_FFN = 512
BS_RET = 32
BS_MIX = 64
VMEM_LIMIT = 56 * 1024 * 1024


def _sigmoid(x):
    return 1.0 / (1.0 + jnp.exp(-x))


def _gelu(x):
    c = math.sqrt(2.0 / math.pi)
    return x * (0.5 + 0.5 * jnp.tanh(x * (c + (c * 0.044715) * (x * x))))


def _rms(x, g):
    ms = jnp.mean(x * x, axis=-1, keepdims=True)
    return x * lax.rsqrt(ms + EPS) * g


def _dot(a, b):
    return jnp.dot(a, b, preferred_element_type=F32)


def _dot_blocks(a, w_refs, blocks=None):
    first, count = blocks if blocks is not None else (0, len(w_refs))
    parts = [_dot(a, w_refs[first + i][...]) for i in range(count)]
    return parts[0] if count == 1 else jnp.concatenate(parts, axis=1)


def _dot_nt(a, b):
    return lax.dot_general(a, b, (((1,), (1,)), ((), ())), preferred_element_type=F32)


def _dot_tn(a, b):
    return lax.dot_general(a, b, (((0,), (0,)), ((), ())), preferred_element_type=F32)


def _rotary_slab(x, cos, sin_signed):
    lane = lax.broadcasted_iota(jnp.int32, x.shape, x.ndim - 1)
    first_half = (lane & (RET_DK - 1)) < (RET_DK // 2)
    other = jnp.where(first_half,
                      pltpu.roll(x, LANES - RET_DK // 2, axis=x.ndim - 1),
                      pltpu.roll(x, RET_DK // 2, axis=x.ndim - 1))
    return x * cos + other * sin_signed


def _token_major(ref):
    return jnp.concatenate([ref[:, t, :] for t in range(ref.shape[1])], axis=0)


def _gate_preacts(xc, wri_ref):
    xcb = xc.astype(BF16)
    return [_dot(xcb[:, g * LRU_GROUP:(g + 1) * LRU_GROUP], wri_ref[g]) for g in range(N_LRU_GROUPS)]


def _lru_gates(xc, pre, br_ref, bi_ref, lam_ref):
    pre_r = [p[:, :LRU_GROUP] for p in pre]
    pre_i = [p[:, LRU_GROUP:] for p in pre]
    r = _sigmoid(jnp.concatenate(pre_r, axis=1) + br_ref[...])
    ig = _sigmoid(jnp.concatenate(pre_i, axis=1) + bi_ref[...])
    lam = lam_ref[...]
    softplus_neg_lam = jnp.maximum(-lam, 0.0) + jnp.log1p(jnp.exp(-jnp.abs(lam)))
    log_a = (-LRU_C * r) * softplus_neg_lam
    a = jnp.exp(log_a)
    u = jnp.sqrt(1.0 - a * a) * (ig * xc)
    return a, u


def _scan_within_groups(a3, u3):
    row = lax.broadcasted_iota(jnp.int32, (1,) + a3.shape[1:], 1)
    for s in (1, 2, 4):
        valid = row >= s
        a_prev = jnp.where(valid, pltpu.roll(a3, s, axis=1), 1.0)
        u_prev = jnp.where(valid, pltpu.roll(u3, s, axis=1), 0.0)
        u3 = a3 * u_prev + u3
        a3 = a3 * a_prev
    return a3, u3


def _head_norm_gate(o, rg, gng_ref, gnb_ref):
    parts = []
    for h in range(RET_HEADS):
        oh = o[:, h * RET_DV:(h + 1) * RET_DV]
        mu = jnp.mean(oh, axis=-1, keepdims=True)
        ctr = oh - mu
        var = jnp.mean(ctr * ctr, axis=-1, keepdims=True)
        parts.append(ctr * lax.rsqrt(var + EPS))
    y = jnp.concatenate(parts, axis=1) * gng_ref[...] + gnb_ref[...]
    return y * (rg * _sigmoid(rg))


def _merge_out(x, nxb, ya, yb, win_ref, wo_ref):
    ga = _dot_blocks(nxb, win_ref, BLK_GA)
    gb = _dot_blocks(nxb, win_ref, BLK_GB)
    merged = _sigmoid(ga) * ya + _sigmoid(gb) * yb
    return x + _dot_blocks(merged.astype(BF16), wo_ref)


def _mixer_prompt_kernel(x_ref, cos_ref, sin_ref, cast_in_refs, dmask_ref, qdec_ref, kdec_ref, cdec_ref,
                         gmix_ref, win_ref, wconv_ref, bconv_ref, wri_ref, br_ref, bi_ref, lam_ref,
                         wlo_ref, gng_ref, gnb_ref, wro_ref, wo_ref,
                         x1_ref, convst_ref, hst_ref, sst_ref, cast_out_refs,
                         proj_ref, hc_ref, s_ref):
    T = T_MIX
    t = pl.program_id(1)


    @pl.when(t == 0)
    def _():
        proj_ref[0:SUBLANES, :] = jnp.zeros((SUBLANES, proj_ref.shape[1]), F32)
        hc_ref[...] = jnp.zeros_like(hc_ref)
        s_ref[...] = jnp.zeros_like(s_ref)

    x = x_ref[...]
    nxb = _rms(x, gmix_ref[...]).astype(BF16)

    def project(first, count):
        for blk in range(first, first + count):
            proj_ref[SUBLANES:SUBLANES + T, blk * W_BLOCK:(blk + 1) * W_BLOCK] = _dot(nxb, win_ref[blk][...])

    project(0, BLK_Q[0])

    def proj(blocks):
        first, count = blocks
        return proj_ref[SUBLANES:SUBLANES + T, first * W_BLOCK:(first + count) * W_BLOCK]

    lx_cols = slice(BLK_LX[0] * W_BLOCK, (BLK_LX[0] + BLK_LX[1]) * W_BLOCK)
    xc = bconv_ref[...] + proj(BLK_LX) * wconv_ref[LRU_CONV - 1:LRU_CONV, :]
    for j in range(LRU_CONV - 1):
        start = SUBLANES - (LRU_CONV - 1) + j
        xc = xc + proj_ref[start:start + T, lx_cols] * wconv_ref[j:j + 1, :]
    pre = _gate_preacts(xc, wri_ref)
    project(BLK_Q[0], len(win_ref) - BLK_Q[0])
    a, u = _lru_gates(xc, pre, br_ref, bi_ref, lam_ref)
    G = T // SUBLANES
    A3, B3 = _scan_within_groups(a.reshape(G, SUBLANES, D_LRU), u.reshape(G, SUBLANES, D_LRU))
    h_in = hc_ref[...]
    hs_groups = []
    for g in range(G):
        h = A3[g] * h_in + B3[g]
        hs_groups.append(h)
        h_in = jnp.broadcast_to(h[SUBLANES - 1:SUBLANES, :], (SUBLANES, D_LRU))
    h_last = h_in
    hc_ref[...] = h_last
    hs = jnp.concatenate(hs_groups, axis=0)
    ya = _dot_blocks((hs * _gelu(proj(BLK_LG))).astype(BF16), wlo_ref)

    for src, dst in zip(cast_in_refs, cast_out_refs):
        dst[...] = src[...].astype(BF16)

    q = proj(BLK_Q)
    k = proj(BLK_K)
    v = proj(BLK_V)
    cos = cos_ref[...]
    sin_signed = sin_ref[...]
    lane = lax.broadcasted_iota(jnp.int32, (T, LANES), 1)
    o_parts = []
    for j in range(N_PAIRS):
        sl = slice(j * LANES, (j + 1) * LANES)
        qr = _rotary_slab(q[:, sl], cos, sin_signed)
        kr = _rotary_slab(k[:, sl], cos, sin_signed) * (RET_DK ** -0.5)
        kb = kr.astype(BF16)
        s_pair = s_ref[j]
        s_pair_b = s_pair.astype(BF16)
        for half in range(2):
            h = 2 * j + half
            qm = jnp.where((lane >> 6) == half, qr, 0.0).astype(BF16)
            p = (_dot_nt(qm, kb) * dmask_ref[h]).astype(BF16)
            vb = v[:, h * RET_DV:(h + 1) * RET_DV].astype(BF16)
            o_parts.append(_dot(p, vb) + _dot(qm, s_pair_b) * qdec_ref[:, h * RET_DV:(h + 1) * RET_DV])
        kd = (kr * kdec_ref[:, sl]).astype(BF16)
        upd = _dot_tn(kd, v[:, 2 * j * RET_DV:(2 * j + 2) * RET_DV].astype(BF16))
        s_ref[j] = s_pair * cdec_ref[j] + jnp.concatenate(
            [upd[0:RET_DK, 0:RET_DV], upd[RET_DK:2 * RET_DK, RET_DV:2 * RET_DV]], axis=0)
    o = jnp.concatenate(o_parts, axis=1)
    yb = _dot_blocks(_head_norm_gate(o, proj(BLK_RG), gng_ref, gnb_ref).astype(BF16), wro_ref)

    merged = _sigmoid(proj(BLK_GA)) * ya + _sigmoid(proj(BLK_GB)) * yb
    x1_ref[...] = x + _dot_blocks(merged.astype(BF16), wo_ref)

    proj_ref[0:SUBLANES, lx_cols] = proj_ref[T:T + SUBLANES, lx_cols]

    @pl.when(t == pl.num_programs(1) - 1)
    def _():
        convst_ref[...] = proj_ref[SUBLANES - (LRU_CONV - 1):SUBLANES, lx_cols]
        hst_ref[...] = h_last[0:1, :]
        sst_ref[...] = s_ref[...]


def _ret_sample_kernel(x_ref, st_ref, cos_ref, sin_ref, d64_ref, qdec_ref, kdec_ref,
                       gmix_ref, wqkv_ref,
                       o_ref, so_ref,
                       q_ref, k_ref, v_ref):
    bs = BS_RET
    nq = RET_HEADS * RET_DK
    nxb = _rms(x_ref[...], gmix_ref[...]).astype(BF16)
    qkv = _dot_blocks(nxb, wqkv_ref)
    cos = cos_ref[...][None]
    sin_signed = sin_ref[...][None]
    for j in range(N_PAIRS):
        sl = slice(j * LANES, (j + 1) * LANES)
        q3 = qkv[:, j * LANES:(j + 1) * LANES].reshape(bs, SUBLANES, LANES)
        k3 = qkv[:, nq + j * LANES:nq + (j + 1) * LANES].reshape(bs, SUBLANES, LANES)
        q_ref[:, sl] = _rotary_slab(q3, cos, sin_signed).reshape(bs * SUBLANES, LANES)
        k_ref[:, sl] = (_rotary_slab(k3, cos, sin_signed) * (RET_DK ** -0.5)).reshape(bs * SUBLANES, LANES)
    v_ref[...] = qkv[:, 2 * nq:]

    lane = lax.broadcasted_iota(jnp.int32, (SUBLANES, nq), 1)
    gammas = [1.0 - 2.0 ** (-5.0 - h) for h in range(RET_HEADS)]

    def per_seq(s, carry):
        r0 = pl.multiple_of(s * SUBLANES, SUBLANES)
        qs = q_ref[pl.ds(r0, SUBLANES), :]
        ks = k_ref[pl.ds(r0, SUBLANES), :]
        vs = v_ref[pl.ds(r0, SUBLANES), :]
        vb = vs.astype(BF16)
        s0 = st_ref[s]
        qbd = jnp.concatenate([jnp.where((lane >> 6) == h, qs, 0.0) for h in range(RET_HEADS)],
                              axis=0).astype(BF16)
        o_state = _dot(qbd, s0.reshape(RET_HEADS * RET_DK, RET_DV).astype(BF16)) * qdec_ref[...]
        p = (_dot_nt(qbd, ks.astype(BF16)) * d64_ref[...]).astype(BF16)
        o_intra = _dot(p, vb)
        o_ref[pl.ds(r0, SUBLANES), :] = jnp.concatenate(
            [o_intra[h * SUBLANES:(h + 1) * SUBLANES, h * RET_DV:(h + 1) * RET_DV]
             + o_state[h * SUBLANES:(h + 1) * SUBLANES, :] for h in range(RET_HEADS)], axis=1)
        kd = (ks * kdec_ref[...]).astype(BF16)
        for j in range(N_PAIRS):
            upd = _dot_tn(kd[:, j * LANES:(j + 1) * LANES], vb[:, 2 * j * RET_DV:(2 * j + 2) * RET_DV])
            so_ref[s, 2 * j] = s0[2 * j] * (gammas[2 * j] ** SUBLANES) + upd[0:RET_DK, 0:RET_DV]
            so_ref[s, 2 * j + 1] = (s0[2 * j + 1] * (gammas[2 * j + 1] ** SUBLANES)
                                    + upd[RET_DK:2 * RET_DK, RET_DV:2 * RET_DV])
        return carry

    lax.fori_loop(0, bs, per_seq, 0, unroll=4)


def _mixer_sample_kernel(x_ref, o_ref, cst_ref, h0_ref,
                         gmix_ref, win_ref, wconv_ref, bconv_ref, wri_ref, br_ref, bi_ref, lam_ref,
                         wlo_ref, gng_ref, gnb_ref, wro_ref, wo_ref,
                         x1_ref, convst_ref, hst_ref):
    bs, steps = x_ref.shape[0], x_ref.shape[1]
    x = _token_major(x_ref)
    nxb = _rms(x, gmix_ref[...]).astype(BF16)

    lx = _dot_blocks(nxb, win_ref, BLK_LX)
    keep = LRU_CONV - 1
    ext = ([cst_ref[:, j * D_LRU:(j + 1) * D_LRU] for j in range(keep)]
           + [lx[t * bs:(t + 1) * bs] for t in range(steps)])
    xc = jnp.concatenate(
        [bconv_ref[...] + sum(ext[t + j] * wconv_ref[j:j + 1, :] for j in range(LRU_CONV)) for t in range(steps)],
        axis=0)
    for j in range(keep):
        convst_ref[j] = ext[steps + j]
    a, u = _lru_gates(xc, _gate_preacts(xc, wri_ref), br_ref, bi_ref, lam_ref)
    h = h0_ref[...]
    hs = []
    for t in range(steps):
        h = a[t * bs:(t + 1) * bs] * h + u[t * bs:(t + 1) * bs]
        hs.append(h)
    hst_ref[...] = h
    lg = _dot_blocks(nxb, win_ref, BLK_LG)
    ya = _dot_blocks((jnp.concatenate(hs, axis=0) * _gelu(lg)).astype(BF16), wlo_ref)

    rg = _dot_blocks(nxb, win_ref, BLK_RG)
    o = _token_major(o_ref)
    yb = _dot_blocks(_head_norm_gate(o, rg, gng_ref, gnb_ref).astype(BF16), wro_ref)

    x1_ref[...] = _merge_out(x, nxb, ya, yb, win_ref, wo_ref).reshape(steps, bs, D_MODEL)


def _ffn_kernel(x_ref, p_ref, *rest, decode):
    keep = FFN_CONV - 1
    if decode:
        (cst_ref, gffn_ref, wup_ref, wconv_ref, bconv_ref, wdown_ref, wple_ref, gple_ref, wgate_ref, gfin_ref,
         y_ref, convst_ref) = rest
        steps, bs = x_ref.shape[0], x_ref.shape[1]
        x = x_ref[...].reshape(steps * bs, D_MODEL)
        p = _token_major(p_ref)
    else:
        (gffn_ref, wup_ref, wconv_ref, bconv_ref, wdown_ref, wple_ref, gple_ref, wgate_ref, gfin_ref,
         y_ref, convst_ref, z_ref, carry_ref) = rest
        t = pl.program_id(1)

        @pl.when(t == 0)
        def _():
            carry_ref[...] = jnp.zeros_like(carry_ref)

        x = x_ref[...]
        p = p_ref[...]
    rows = x.shape[0]
    nxb = _rms(x, gffn_ref[...]).astype(BF16)
    n_chunks = D_FF // FFN_CHUNK
    act_chunks = []
    acc = None
    for c in range(n_chunks):
        halves = []
        for half in range(2):
            blk = half * n_chunks + c
            cols = slice(blk * FFN_CHUNK, (blk + 1) * FFN_CHUNK)
            up = _dot(nxb, wup_ref[blk][...])
            w = wconv_ref[:, cols]
            bias = bconv_ref[:, cols]
            if decode:
                ext = ([cst_ref[:, j * 2 * D_FF + cols.start:j * 2 * D_FF + cols.stop] for j in range(keep)]
                       + [up[s * bs:(s + 1) * bs] for s in range(steps)])
                halves.append(jnp.concatenate(
                    [bias + sum(ext[s + j] * w[j:j + 1] for j in range(FFN_CONV)) for s in range(steps)], axis=0))
                for j in range(keep):
                    convst_ref[j, :, cols] = ext[steps + j]
            else:
                zc_ref = z_ref.at[2 * (c % 2) + half]
                zc_ref[0:SUBLANES, :] = carry_ref[:, cols]
                zc_ref[SUBLANES:SUBLANES + rows, :] = up
                y = bias + up * w[keep:keep + 1]
                for j in range(keep):
                    start = SUBLANES - keep + j
                    y = y + zc_ref[start:start + rows, :] * w[j:j + 1]
                carry_ref[:, cols] = zc_ref[rows:rows + SUBLANES, :]
                halves.append(y)
        act_chunks.append((_gelu(halves[0]) * halves[1]).astype(BF16))
        if len(act_chunks) == DOWN_GROUP:
            k0 = (c + 1 - DOWN_GROUP) * FFN_CHUNK
            part = jnp.concatenate(
                [_dot(jnp.concatenate(act_chunks, axis=1), w[k0:k0 + DOWN_GROUP * FFN_CHUNK, :]) for w in wdown_ref],
                axis=1)
            acc = part if acc is None else acc + part
            act_chunks = []
    x2 = x + acc

    e = _rms(_dot_blocks(p.astype(BF16), wple_ref), gple_ref[...])
    x3 = x2 + _sigmoid(_dot_blocks(x2.astype(BF16), wgate_ref)) * e
    y = _rms(x3, gfin_ref[...])

    if decode:
        for s in range(steps):
            y_ref[:, s, :] = y[s * bs:(s + 1) * bs]
    else:
        y_ref[...] = y

        @pl.when(t == pl.num_programs(1) - 1)
        def _():
            convst_ref[...] = carry_ref[SUBLANES - keep:SUBLANES, :]


def _const_spec(shape):
    n = len(shape)
    return pl.BlockSpec(shape, lambda *_: (0,) * n, pipeline_mode=pl.Buffered(1))


def _row(v):
    return v.reshape(1, -1)


def _rotary_tables(first_pos, n):
    half = RET_DK // 2
    inv = ROPE_BASE ** (-np.arange(half, dtype=np.float64) / half)
    ang = (first_pos + np.arange(n, dtype=np.float64))[:, None] * inv[None, :]
    cos, sin = np.cos(ang), np.sin(ang)
    cos_t = np.tile(cos, (1, LANES // half))
    sin_t = np.tile(np.concatenate([-sin, sin], axis=1), (1, LANES // RET_DK))
    return jnp.asarray(cos_t, F32), jnp.asarray(sin_t, F32)


def _decay_tables(chunk):
    log_gamma = np.log1p(-(2.0 ** (-5.0 - np.arange(RET_HEADS, dtype=np.float64))))
    idx = np.arange(chunk)
    rel = idx[:, None] - idx[None, :]
    dmask = np.where(rel[None] >= 0, np.exp(log_gamma[:, None, None] * np.maximum(rel, 0)[None]), 0.0)
    q_decay = np.exp(log_gamma[None, :] * (idx + 1)[:, None])
    k_decay = np.exp(log_gamma[None, :] * (chunk - 1 - idx)[:, None])
    chunk_decay = np.exp(log_gamma * chunk)
    return dmask, q_decay, k_decay, chunk_decay


def _block_diag_gates(w_r, w_i):
    per = LRU_GROUP // LRU_BLOCK
    eye = jnp.eye(per, dtype=w_r.dtype)

    def pack(w):
        w4 = w.reshape(N_LRU_GROUPS, per, LRU_BLOCK, LRU_BLOCK)
        return jnp.einsum('gbij,bc->gbicj', w4, eye).reshape(N_LRU_GROUPS, LRU_GROUP, LRU_GROUP)

    return jnp.concatenate([pack(w_r), pack(w_i)], axis=-1).astype(BF16)


def _whole(a):
    return ([a], [_const_spec(a.shape)], None)


def _col_blocks(w, first=0, count=None):
    k, n = w.shape
    count = n // W_BLOCK - first if count is None else count
    wb = w.astype(BF16)
    specs = [pl.BlockSpec((k, W_BLOCK), functools.partial(lambda j, *_: (0, j), first + i),
                          pipeline_mode=pl.Buffered(1)) for i in range(count)]
    return ([wb] * count, specs, count)


def _flatten(groups):
    arrays = [a for g in groups for a in g[0]]
    specs = [sp for g in groups for sp in g[1]]
    return arrays, specs, [g[2] for g in groups]


def _grouped(body, counts):
    def kernel_fn(*refs):
        args, i = [], 0
        for n in counts:
            if n is None:
                args.append(refs[i])
                i += 1
            else:
                args.append(list(refs[i:i + n]))
                i += n
        assert i == len(refs)
        return body(*args)
    return kernel_fn


def _one(a, spec):
    return ([a], [spec], None)


def _call(body, name, grid, operands, out_specs, out_shape, scratch_shapes=(), out_counts=None):
    arrays, specs, counts = _flatten(operands)
    counts = counts + (out_counts or [None] * len(out_shape)) + [None] * len(scratch_shapes)
    return pl.pallas_call(
        _grouped(body, counts), grid=grid, in_specs=specs, out_specs=out_specs, out_shape=out_shape,
        scratch_shapes=list(scratch_shapes),
        compiler_params=pltpu.CompilerParams(dimension_semantics=("arbitrary",) * len(grid),
                                             vmem_limit_bytes=VMEM_LIMIT),
        name=name)(*arrays)


def kernel(x_prompt, x_sample, p_prompt, p_sample, state_lru_conv, state_lru_h, state_ret, state_ffn_conv, g_mix, w_in, w_lru_conv, b_lru_conv, w_r, b_r, w_i, b_i, lru_lambda, w_lru_out, gn_g, gn_b, w_ret_out, w_o, g_ffn, w_up, w_ffn_conv, b_ffn_conv, w_down, w_ple, g_ple, w_ple_gate, g_final):
    depth = g_mix.shape[0]
    assert depth == 1
    bp, lp, _ = x_prompt.shape
    bsq, ls, _ = x_sample.shape
    assert ls == SUBLANES and lp % T_MIX == 0 and lp % T_FFN == 0
    assert bsq % BS_RET == 0 and bsq % BS_MIX == 0
    hdv = RET_HEADS * RET_DV
    hdk = RET_HEADS * RET_DK

    mixer_weights = [_whole(_row(g_mix[0])), _col_blocks(w_in[0]), _whole(w_lru_conv[0]), _whole(_row(b_lru_conv[0])),
                     _whole(_block_diag_gates(w_r[0], w_i[0])), _whole(_row(b_r[0])), _whole(_row(b_i[0])),
                     _whole(_row(lru_lambda[0])), _col_blocks(w_lru_out[0]), _whole(_row(gn_g[0])),
                     _whole(_row(gn_b[0])), _col_blocks(w_ret_out[0]), _col_blocks(w_o[0])]
    n_steps = bp * (lp // T_MIX)
    to_cast = [w_up[0], w_down[0], w_ple_gate[0]]
    slab = lambda w: pl.BlockSpec((w.shape[0] // n_steps, w.shape[1]), lambda b, t: (b * (lp // T_MIX) + t, 0))
    assert all(w.shape[0] % n_steps == 0 and (w.shape[0] // n_steps) % BF16_ROWS == 0 for w in to_cast)

    cos_p, sin_p = _rotary_tables(0, lp)
    dmask, q_decay, k_decay, chunk_decay = _decay_tables(T_MIX)
    dmask = jnp.asarray(dmask, F32)
    qdec_p = jnp.asarray(np.repeat(q_decay, RET_DV, axis=1), F32)
    kdec_p = jnp.asarray(np.repeat(k_decay, RET_DK, axis=1), F32)
    cdec_p = jnp.asarray(np.broadcast_to(np.repeat(chunk_decay, RET_DK).reshape(N_PAIRS, 2 * RET_DK, 1),
                                         (N_PAIRS, 2 * RET_DK, RET_DV)), F32)
    tile = lambda w: pl.BlockSpec((None, T_MIX, w), lambda b, t: (b, t, 0))
    table = pl.BlockSpec((T_MIX, LANES), lambda b, t: (t, 0))
    x1_p, convst_p, hst_p, sst_p, wup_b, wdown_b, wgate_b = _call(
        _mixer_prompt_kernel, "mixer_prompt", (bp, lp // T_MIX),
        [_one(x_prompt, tile(D_MODEL)), _one(cos_p, table), _one(sin_p, table),
         (to_cast, [slab(w) for w in to_cast], len(to_cast)),
         _whole(dmask), _whole(qdec_p), _whole(kdec_p), _whole(cdec_p)] + mixer_weights,
        out_specs=[tile(D_MODEL),
                   pl.BlockSpec((None, LRU_CONV - 1, D_LRU), lambda b, t: (b, 0, 0)),
                   pl.BlockSpec((None, 1, D_LRU), lambda b, t: (b, 0, 0)),
                   pl.BlockSpec((None, N_PAIRS, 2 * RET_DK, RET_DV), lambda b, t: (b, 0, 0, 0))]
                  + [slab(w) for w in to_cast],
        out_shape=[jax.ShapeDtypeStruct((bp, lp, D_MODEL), F32),
                   jax.ShapeDtypeStruct((bp, LRU_CONV - 1, D_LRU), F32),
                   jax.ShapeDtypeStruct((bp, 1, D_LRU), F32),
                   jax.ShapeDtypeStruct((bp, N_PAIRS, 2 * RET_DK, RET_DV), F32)]
                  + [jax.ShapeDtypeStruct(w.shape, BF16) for w in to_cast],
        out_counts=[None] * 4 + [len(to_cast)],
        scratch_shapes=[pltpu.VMEM((T_MIX + 2 * SUBLANES, w_in.shape[2]), F32), pltpu.VMEM((SUBLANES, D_LRU), F32),
                        pltpu.VMEM((N_PAIRS, 2 * RET_DK, RET_DV), F32)])

    ffn_weights = [_whole(_row(g_ffn[0])), _col_blocks(wup_b), _whole(w_ffn_conv[0]), _whole(_row(b_ffn_conv[0])),
                   _col_blocks(wdown_b), _col_blocks(w_ple[0]), _whole(_row(g_ple[0])),
                   _col_blocks(wgate_b), _whole(_row(g_final))]
    tilef = lambda w: pl.BlockSpec((None, T_FFN, w), lambda b, t: (b, t, 0))
    y_p, ffnst_p = _call(
        functools.partial(_ffn_kernel, decode=False), "ffn_prompt", (bp, lp // T_FFN),
        [_one(x1_p, tilef(D_MODEL)), _one(p_prompt[0], tilef(PLE_DIM))] + ffn_weights,
        out_specs=[tilef(D_MODEL), pl.BlockSpec((None, FFN_CONV - 1, 2 * D_FF), lambda b, t: (b, 0, 0))],
        out_shape=[jax.ShapeDtypeStruct((bp, lp, D_MODEL), F32),
                   jax.ShapeDtypeStruct((bp, FFN_CONV - 1, 2 * D_FF), F32)],
        scratch_shapes=[pltpu.VMEM((4, T_FFN + 2 * SUBLANES, FFN_CHUNK), F32),
                        pltpu.VMEM((SUBLANES, 2 * D_FF), F32)])

    rows_s = bsq * ls
    cos_s, sin_s = _rotary_tables(PAST_LEN, ls)
    dmask8, q_decay8, k_decay8, _ = _decay_tables(ls)
    d64 = jnp.asarray(dmask8.reshape(RET_HEADS * ls, ls), F32)
    qdec64 = jnp.asarray(np.broadcast_to(q_decay8.T.reshape(RET_HEADS * ls, 1), (RET_HEADS * ls, RET_DV)), F32)
    kdec8 = jnp.asarray(np.repeat(k_decay8, RET_DK, axis=1), F32)
    rr = BS_RET * ls
    state_block = pl.BlockSpec((BS_RET, RET_HEADS, RET_DK, RET_DV), lambda i: (i, 0, 0, 0))
    o_s, sst_s = _call(
        _ret_sample_kernel, "ret_sample", (bsq // BS_RET,),
        [_one(x_sample.reshape(rows_s, D_MODEL), pl.BlockSpec((rr, D_MODEL), lambda i: (i, 0))),
         _one(state_ret[0], state_block),
         _whole(cos_s), _whole(sin_s), _whole(d64), _whole(qdec64), _whole(kdec8), _whole(_row(g_mix[0])),
         _col_blocks(w_in[0], BLK_Q[0], BLK_RG[0] - BLK_Q[0])],
        out_specs=[pl.BlockSpec((rr, hdv), lambda i: (i, 0)), state_block],
        out_shape=[jax.ShapeDtypeStruct((rows_s, hdv), F32),
                   jax.ShapeDtypeStruct((bsq, RET_HEADS, RET_DK, RET_DV), F32)],
        scratch_shapes=[pltpu.VMEM((rr, hdk), F32), pltpu.VMEM((rr, hdk), F32), pltpu.VMEM((rr, hdv), F32)])

    seqs = lambda *dims: pl.BlockSpec((BS_MIX,) + dims, lambda i: (i,) + (0,) * len(dims))
    by_row = lambda rows, c: pl.BlockSpec((rows, BS_MIX, c), lambda i: (0, i, 0))
    keep_l, keep_f = LRU_CONV - 1, FFN_CONV - 1
    x1_s, convst_s, hst_s = _call(
        _mixer_sample_kernel, "mixer_sample", (bsq // BS_MIX,),
        [_one(x_sample, seqs(ls, D_MODEL)), _one(o_s.reshape(bsq, ls, hdv), seqs(ls, hdv)),
         _one(state_lru_conv[0].reshape(bsq, keep_l * D_LRU), seqs(keep_l * D_LRU)),
         _one(state_lru_h[0], seqs(D_LRU))] + mixer_weights,
        out_specs=[by_row(ls, D_MODEL), by_row(keep_l, D_LRU), seqs(D_LRU)],
        out_shape=[jax.ShapeDtypeStruct((ls, bsq, D_MODEL), F32),
                   jax.ShapeDtypeStruct((keep_l, bsq, D_LRU), F32),
                   jax.ShapeDtypeStruct((bsq, D_LRU), F32)])

    y_s, ffnst_s = _call(
        functools.partial(_ffn_kernel, decode=True), "ffn_sample", (bsq // BS_MIX,),
        [_one(x1_s, by_row(ls, D_MODEL)), _one(p_sample[0], seqs(ls, PLE_DIM)),
         _one(state_ffn_conv[0].reshape(bsq, keep_f * 2 * D_FF), seqs(keep_f * 2 * D_FF))] + ffn_weights,
        out_specs=[seqs(ls, D_MODEL), by_row(keep_f, 2 * D_FF)],
        out_shape=[jax.ShapeDtypeStruct((bsq, ls, D_MODEL), F32),
                   jax.ShapeDtypeStruct((keep_f, bsq, 2 * D_FF), F32)])

    return (y_p,
            y_s,
            convst_p[None],
            hst_p.reshape(1, bp, D_LRU),
            sst_p.reshape(1, bp, RET_HEADS, RET_DK, RET_DV),
            ffnst_p[None],
            jnp.swapaxes(convst_s, 0, 1)[None],
            hst_s[None],
            sst_s[None],
            jnp.swapaxes(ffnst_s, 0, 1)[None])
```

```python
import functools
import math

import jax
import jax.numpy as jnp
import numpy as np
from jax import lax
from jax.experimental import pallas as pl
from jax.experimental.pallas import tpu as pltpu

F32 = jnp.float32
BF16 = jnp.bfloat16

D_MODEL = 1024
PLE_DIM = 256
D_LRU = 1024
LRU_BLOCK = 64
LRU_GROUP = 256
N_LRU_GROUPS = D_LRU // LRU_GROUP
LRU_CONV = 4
LRU_C = 8.0
RET_HEADS = 8
RET_DK = 64
RET_DV = 128
N_PAIRS = RET_HEADS // 2
ROPE_BASE = 10000.0
D_FF = 3072
FFN_CONV = 3
FFN_CHUNK = 512
DOWN_GROUP = 2
EPS = 1e-6
PAST_LEN = 16384
SUBLANES = 8
BF16_ROWS = 16
LANES = 128

W_BLOCK = 512
BLK_LX, BLK_LG, BLK_Q, BLK_K, BLK_V, BLK_RG, BLK_GA, BLK_GB = (0, 2), (2, 2), (4, 1), (5, 1), (6, 2), (8, 2), (10, 2), (12, 2)

T_MIX = 256
T_FFN = 512
RET_UNROLL = 4
BS_RET = 32
BS_MIX = 64
VMEM_LIMIT = 56 * 1024 * 1024


def _sigmoid(x):
    return 1.0 / (1.0 + jnp.exp(-x))


def _gelu(x):
    c = math.sqrt(2.0 / math.pi)
    return x * (0.5 + 0.5 * jnp.tanh(x * (c + (c * 0.044715) * (x * x))))


def _rms(x, g):
    ms = jnp.mean(x * x, axis=-1, keepdims=True)
    return x * lax.rsqrt(ms + EPS) * g


def _dot(a, b):
    return jnp.dot(a, b, preferred_element_type=F32)


def _dot_blocks(a, w_refs, blocks=None):
    first, count = blocks if blocks is not None else (0, len(w_refs))
    parts = [_dot(a, w_refs[first + i][...]) for i in range(count)]
    return parts[0] if count == 1 else jnp.concatenate(parts, axis=1)


def _dot_row_halves(a, w):
    half = a.shape[0] // 2
    return jnp.concatenate([_dot(a[:half], w), _dot(a[half:], w)], axis=0)


def _dot_nt(a, b):
    return lax.dot_general(a, b, (((1,), (1,)), ((), ())), preferred_element_type=F32)


def _dot_tn(a, b):
    return lax.dot_general(a, b, (((0,), (0,)), ((), ())), preferred_element_type=F32)


def _rotary_slab(x, cos, sin_signed):
    lane = lax.broadcasted_iota(jnp.int32, x.shape, x.ndim - 1)
    first_half = (lane & (RET_DK - 1)) < (RET_DK // 2)
    other = jnp.where(first_half,
                      pltpu.roll(x, LANES - RET_DK // 2, axis=x.ndim - 1),
                      pltpu.roll(x, RET_DK // 2, axis=x.ndim - 1))
    return x * cos + other * sin_signed


def _token_major(ref):
    return jnp.concatenate([ref[:, t, :] for t in range(ref.shape[1])], axis=0)


def _gate_preacts(xc, wri_ref):
    xcb = xc.astype(BF16)
    return [_dot(xcb[:, g * LRU_GROUP:(g + 1) * LRU_GROUP], wri_ref[g]) for g in range(N_LRU_GROUPS)]


def _lru_gates(xc, pre, br_ref, bi_ref, lam_ref):
    pre_r = [p[:, :LRU_GROUP] for p in pre]
    pre_i = [p[:, LRU_GROUP:] for p in pre]
    r = _sigmoid(jnp.concatenate(pre_r, axis=1) + br_ref[...])
    ig = _sigmoid(jnp.concatenate(pre_i, axis=1) + bi_ref[...])
    lam = lam_ref[...]
    softplus_neg_lam = jnp.maximum(-lam, 0.0) + jnp.log1p(jnp.exp(-jnp.abs(lam)))
    log_a = (-LRU_C * r) * softplus_neg_lam
    a = jnp.exp(log_a)
    u = jnp.sqrt(1.0 - a * a) * (ig * xc)
    return a, u


def _scan_within_groups(a3, u3):
    row = lax.broadcasted_iota(jnp.int32, (1,) + a3.shape[1:], 1)
    for s in (1, 2, 4):
        valid = row >= s
        a_prev = jnp.where(valid, pltpu.roll(a3, s, axis=1), 1.0)
        u_prev = jnp.where(valid, pltpu.roll(u3, s, axis=1), 0.0)
        u3 = a3 * u_prev + u3
        a3 = a3 * a_prev
    return a3, u3


def _head_norm_gate(o, rg, gng_ref, gnb_ref):
    parts = []
    for h in range(RET_HEADS):
        oh = o[:, h * RET_DV:(h + 1) * RET_DV]
        mu = jnp.mean(oh, axis=-1, keepdims=True)
        ctr = oh - mu
        var = jnp.mean(ctr * ctr, axis=-1, keepdims=True)
        parts.append(ctr * lax.rsqrt(var + EPS))
    y = jnp.concatenate(parts, axis=1) * gng_ref[...] + gnb_ref[...]
    return y * (rg * _sigmoid(rg))


def _merge_out(x, nxb, ya, yb, win_ref, wo_ref):
    ga = _dot_blocks(nxb, win_ref, BLK_GA)
    gb = _dot_blocks(nxb, win_ref, BLK_GB)
    merged = _sigmoid(ga) * ya + _sigmoid(gb) * yb
    return x + _dot_blocks(merged.astype(BF16), wo_ref)


def _mixer_prompt_kernel(x_ref, cos_ref, sin_ref, cast_in_refs, dmask_ref, qdec_ref, kdec_ref, cdec_ref,
                         gmix_ref, win_ref, wconv_ref, bconv_ref, wri_ref, br_ref, bi_ref, lam_ref,
                         wlo_ref, gng_ref, gnb_ref, wro_ref, wo_ref,
                         x1_ref, convst_ref, hst_ref, sst_ref, cast_out_refs,
                         proj_ref, hc_ref, s_ref):
    T = T_MIX
    t = pl.program_id(1)


    @pl.when(t == 0)
    def _():
        proj_ref[0:SUBLANES, :] = jnp.zeros((SUBLANES, proj_ref.shape[1]), F32)
        hc_ref[...] = jnp.zeros_like(hc_ref)
        s_ref[...] = jnp.zeros_like(s_ref)

    x = x_ref[...]
    nxb = _rms(x, gmix_ref[...]).astype(BF16)

    def project(first, count):
        for blk in range(first, first + count):
            proj_ref[SUBLANES:SUBLANES + T, blk * W_BLOCK:(blk + 1) * W_BLOCK] = _dot(nxb, win_ref[blk][...])

    project(0, BLK_Q[0])

    def proj(blocks):
        first, count = blocks
        return proj_ref[SUBLANES:SUBLANES + T, first * W_BLOCK:(first + count) * W_BLOCK]

    lx_cols = slice(BLK_LX[0] * W_BLOCK, (BLK_LX[0] + BLK_LX[1]) * W_BLOCK)
    xc = bconv_ref[...] + proj(BLK_LX) * wconv_ref[LRU_CONV - 1:LRU_CONV, :]
    for j in range(LRU_CONV - 1):
        start = SUBLANES - (LRU_CONV - 1) + j
        xc = xc + proj_ref[start:start + T, lx_cols] * wconv_ref[j:j + 1, :]
    pre = _gate_preacts(xc, wri_ref)
    project(BLK_Q[0], len(win_ref) - BLK_Q[0])
    a, u = _lru_gates(xc, pre, br_ref, bi_ref, lam_ref)
    G = T // SUBLANES
    A3, B3 = _scan_within_groups(a.reshape(G, SUBLANES, D_LRU), u.reshape(G, SUBLANES, D_LRU))
    h_in = hc_ref[...]
    hs_groups = []
    for g in range(G):
        h = A3[g] * h_in + B3[g]
        hs_groups.append(h)
        h_in = jnp.broadcast_to(h[SUBLANES - 1:SUBLANES, :], (SUBLANES, D_LRU))
    h_last = h_in
    hc_ref[...] = h_last
    hs = jnp.concatenate(hs_groups, axis=0)
    ya = _dot_blocks((hs * _gelu(proj(BLK_LG))).astype(BF16), wlo_ref)

    for src, dst in zip(cast_in_refs, cast_out_refs):
        dst[...] = src[...].astype(BF16)

    q = proj(BLK_Q)
    k = proj(BLK_K)
    v = proj(BLK_V)
    cos = cos_ref[...]
    sin_signed = sin_ref[...]
    lane = lax.broadcasted_iota(jnp.int32, (T, LANES), 1)
    o_parts = []
    for j in range(N_PAIRS):
        sl = slice(j * LANES, (j + 1) * LANES)
        qr = _rotary_slab(q[:, sl], cos, sin_signed)
        kr = _rotary_slab(k[:, sl], cos, sin_signed) * (RET_DK ** -0.5)
        kb = kr.astype(BF16)
        s_pair = s_ref[j]
        s_pair_b = s_pair.astype(BF16)
        for half in range(2):
            h = 2 * j + half
            qm = jnp.where((lane >> 6) == half, qr, 0.0).astype(BF16)
            p = (_dot_nt(qm, kb) * dmask_ref[h]).astype(BF16)
            vb = v[:, h * RET_DV:(h + 1) * RET_DV].astype(BF16)
            o_parts.append(_dot(p, vb) + _dot(qm, s_pair_b) * qdec_ref[:, h * RET_DV:(h + 1) * RET_DV])
        kd = (kr * kdec_ref[:, sl]).astype(BF16)
        upd = _dot_tn(kd, v[:, 2 * j * RET_DV:(2 * j + 2) * RET_DV].astype(BF16))
        s_ref[j] = s_pair * cdec_ref[j] + jnp.concatenate(
            [upd[0:RET_DK, 0:RET_DV], upd[RET_DK:2 * RET_DK, RET_DV:2 * RET_DV]], axis=0)
    o = jnp.concatenate(o_parts, axis=1)
    yb = _dot_blocks(_head_norm_gate(o, proj(BLK_RG), gng_ref, gnb_ref).astype(BF16), wro_ref)

    merged = _sigmoid(proj(BLK_GA)) * ya + _sigmoid(proj(BLK_GB)) * yb
    x1_ref[...] = x + _dot_blocks(merged.astype(BF16), wo_ref)

    proj_ref[0:SUBLANES, lx_cols] = proj_ref[T:T + SUBLANES, lx_cols]

    @pl.when(t == pl.num_programs(1) - 1)
    def _():
        convst_ref[...] = proj_ref[SUBLANES - (LRU_CONV - 1):SUBLANES, lx_cols]
        hst_ref[...] = h_last[0:1, :]
        sst_ref[...] = s_ref[...]


def _ret_sample_kernel(x_ref, st_ref, cos_ref, sin_ref, d64_ref, qdec_ref, kdec_ref,
                       gmix_ref, wqkv_ref,
                       o_ref, so_ref,
                       q_ref, k_ref, v_ref):
    bs = BS_RET
    nq = RET_HEADS * RET_DK
    nxb = _rms(x_ref[...], gmix_ref[...]).astype(BF16)
    qkv = _dot_blocks(nxb, wqkv_ref)
    cos = cos_ref[...][None]
    sin_signed = sin_ref[...][None]
    for j in range(N_PAIRS):
        sl = slice(j * LANES, (j + 1) * LANES)
        q3 = qkv[:, j * LANES:(j + 1) * LANES].reshape(bs, SUBLANES, LANES)
        k3 = qkv[:, nq + j * LANES:nq + (j + 1) * LANES].reshape(bs, SUBLANES, LANES)
        q_ref[:, sl] = _rotary_slab(q3, cos, sin_signed).reshape(bs * SUBLANES, LANES)
        k_ref[:, sl] = (_rotary_slab(k3, cos, sin_signed) * (RET_DK ** -0.5)).reshape(bs * SUBLANES, LANES)
    v_ref[...] = qkv[:, 2 * nq:]

    lane = lax.broadcasted_iota(jnp.int32, (SUBLANES, nq), 1)
    gammas = [1.0 - 2.0 ** (-5.0 - h) for h in range(RET_HEADS)]

    def first_stage(s):
        r0 = pl.multiple_of(s * SUBLANES, SUBLANES)
        qs = q_ref[pl.ds(r0, SUBLANES), :]
        ks = k_ref[pl.ds(r0, SUBLANES), :]
        qbd = jnp.concatenate([jnp.where((lane >> 6) == h, qs, 0.0) for h in range(RET_HEADS)],
                              axis=0).astype(BF16)
        o_state = _dot(qbd, st_ref[s].reshape(RET_HEADS * RET_DK, RET_DV).astype(BF16)) * qdec_ref[...]
        p = (_dot_nt(qbd, ks.astype(BF16)) * d64_ref[...]).astype(BF16)
        kd = (ks * kdec_ref[...]).astype(BF16)
        return o_state, p, kd

    def second_stage(s, o_state, p, kd):
        r0 = pl.multiple_of(s * SUBLANES, SUBLANES)
        vb = v_ref[pl.ds(r0, SUBLANES), :].astype(BF16)
        s0 = st_ref[s]
        o_intra = _dot(p, vb)
        o_ref[pl.ds(r0, SUBLANES), :] = jnp.concatenate(
            [o_intra[h * SUBLANES:(h + 1) * SUBLANES, h * RET_DV:(h + 1) * RET_DV]
             + o_state[h * SUBLANES:(h + 1) * SUBLANES, :] for h in range(RET_HEADS)], axis=1)
        for j in range(N_PAIRS):
            upd = _dot_tn(kd[:, j * LANES:(j + 1) * LANES], vb[:, 2 * j * RET_DV:(2 * j + 2) * RET_DV])
            so_ref[s, 2 * j] = s0[2 * j] * (gammas[2 * j] ** SUBLANES) + upd[0:RET_DK, 0:RET_DV]
            so_ref[s, 2 * j + 1] = (s0[2 * j + 1] * (gammas[2 * j + 1] ** SUBLANES)
                                    + upd[RET_DK:2 * RET_DK, RET_DV:2 * RET_DV])

    def per_seq(s, carry):
        nxt = first_stage(jnp.minimum(s + 1, bs - 1))
        second_stage(s, *carry)
        return nxt

    lax.fori_loop(0, bs, per_seq, first_stage(0), unroll=RET_UNROLL)


def _mixer_sample_kernel(x_ref, o_ref, cst_ref, h0_ref,
                         gmix_ref, win_ref, wconv_ref, bconv_ref, wri_ref, br_ref, bi_ref, lam_ref,
                         wlo_ref, gng_ref, gnb_ref, wro_ref, wo_ref,
                         x1_ref, convst_ref, hst_ref):
    bs, steps = x_ref.shape[0], x_ref.shape[1]
    x = _token_major(x_ref)
    nxb = _rms(x, gmix_ref[...]).astype(BF16)

    lx = _dot_blocks(nxb, win_ref, BLK_LX)
    keep = LRU_CONV - 1
    ext = ([cst_ref[:, j * D_LRU:(j + 1) * D_LRU] for j in range(keep)]
           + [lx[t * bs:(t + 1) * bs] for t in range(steps)])
    xc = jnp.concatenate(
        [bconv_ref[...] + sum(ext[t + j] * wconv_ref[j:j + 1, :] for j in range(LRU_CONV)) for t in range(steps)],
        axis=0)
    for j in range(keep):
        convst_ref[j] = ext[steps + j]
    a, u = _lru_gates(xc, _gate_preacts(xc, wri_ref), br_ref, bi_ref, lam_ref)
    h = h0_ref[...]
    hs = []
    for t in range(steps):
        h = a[t * bs:(t + 1) * bs] * h + u[t * bs:(t + 1) * bs]
        hs.append(h)
    hst_ref[...] = h
    lg = _dot_blocks(nxb, win_ref, BLK_LG)
    ya = _dot_blocks((jnp.concatenate(hs, axis=0) * _gelu(lg)).astype(BF16), wlo_ref)

    rg = _dot_blocks(nxb, win_ref, BLK_RG)
    o = _token_major(o_ref)
    yb = _dot_blocks(_head_norm_gate(o, rg, gng_ref, gnb_ref).astype(BF16), wro_ref)

    x1_ref[...] = _merge_out(x, nxb, ya, yb, win_ref, wo_ref).reshape(steps, bs, D_MODEL)


def _ffn_kernel(x_ref, p_ref, *rest, decode):
    keep = FFN_CONV - 1
    if decode:
        (cst_ref, gffn_ref, wup_ref, wconv_ref, bconv_ref, wdown_ref, wple_ref, gple_ref, wgate_ref, gfin_ref,
         y_ref, convst_ref) = rest
        steps, bs = x_ref.shape[0], x_ref.shape[1]
        x = x_ref[...].reshape(steps * bs, D_MODEL)
        p = _token_major(p_ref)
    else:
        (gffn_ref, wup_ref, wconv_ref, bconv_ref, wdown_ref, wple_ref, gple_ref, wgate_ref, gfin_ref,
         y_ref, convst_ref, z_ref, carry_ref) = rest
        t = pl.program_id(1)

        @pl.when(t == 0)
        def _():
            carry_ref[...] = jnp.zeros_like(carry_ref)

        x = x_ref[...]
        p = p_ref[...]
    rows = x.shape[0]
    nxb = _rms(x, gffn_ref[...]).astype(BF16)
    n_chunks = D_FF // FFN_CHUNK
    act_chunks = []
    acc = None
    for c in range(n_chunks):
        halves = []
        for half in range(2):
            blk = half * n_chunks + c
            cols = slice(blk * FFN_CHUNK, (blk + 1) * FFN_CHUNK)
            up = (_dot_row_halves if c == 0 and half == 0 else _dot)(nxb, wup_ref[blk][...])
            w = wconv_ref[:, cols]
            bias = bconv_ref[:, cols]
            if decode:
                ext = ([cst_ref[:, j * 2 * D_FF + cols.start:j * 2 * D_FF + cols.stop] for j in range(keep)]
                       + [up[s * bs:(s + 1) * bs] for s in range(steps)])
                halves.append(jnp.concatenate(
                    [bias + sum(ext[s + j] * w[j:j + 1] for j in range(FFN_CONV)) for s in range(steps)], axis=0))
                for j in range(keep):
                    convst_ref[j, :, cols] = ext[steps + j]
            else:
                zc_ref = z_ref.at[2 * (c % 2) + half]
                zc_ref[0:SUBLANES, :] = carry_ref[:, cols]
                zc_ref[SUBLANES:SUBLANES + rows, :] = up
                y = bias + up * w[keep:keep + 1]
                for j in range(keep):
                    start = SUBLANES - keep + j
                    y = y + zc_ref[start:start + rows, :] * w[j:j + 1]
                carry_ref[:, cols] = zc_ref[rows:rows + SUBLANES, :]
                halves.append(y)
        act_chunks.append((_gelu(halves[0]) * halves[1]).astype(BF16))
        if len(act_chunks) == DOWN_GROUP:
            k0 = (c + 1 - DOWN_GROUP) * FFN_CHUNK
            part = jnp.concatenate(
                [_dot(jnp.concatenate(act_chunks, axis=1), w[k0:k0 + DOWN_GROUP * FFN_CHUNK, :]) for w in wdown_ref],
                axis=1)
            acc = part if acc is None else acc + part
            act_chunks = []
    x2 = x + acc

    e = _rms(_dot_blocks(p.astype(BF16), wple_ref), gple_ref[...])
    x2b = x2.astype(BF16)
    x3 = x2 + _sigmoid(jnp.concatenate([_dot_row_halves(x2b, w[...]) for w in wgate_ref], axis=1)) * e
    y = _rms(x3, gfin_ref[...])

    if decode:
        for s in range(steps):
            y_ref[:, s, :] = y[s * bs:(s + 1) * bs]
    else:
        y_ref[...] = y

        @pl.when(t == pl.num_programs(1) - 1)
        def _():
            convst_ref[...] = carry_ref[SUBLANES - keep:SUBLANES, :]


def _const_spec(shape):
    n = len(shape)
    return pl.BlockSpec(shape, lambda *_: (0,) * n, pipeline_mode=pl.Buffered(1))


def _row(v):
    return v.reshape(1, -1)


def _rotary_tables(first_pos, n):
    half = RET_DK // 2
    inv = ROPE_BASE ** (-np.arange(half, dtype=np.float64) / half)
    ang = (first_pos + np.arange(n, dtype=np.float64))[:, None] * inv[None, :]
    cos, sin = np.cos(ang), np.sin(ang)
    cos_t = np.tile(cos, (1, LANES // half))
    sin_t = np.tile(np.concatenate([-sin, sin], axis=1), (1, LANES // RET_DK))
    return jnp.asarray(cos_t, F32), jnp.asarray(sin_t, F32)


def _decay_tables(chunk):
    log_gamma = np.log1p(-(2.0 ** (-5.0 - np.arange(RET_HEADS, dtype=np.float64))))
    idx = np.arange(chunk)
    rel = idx[:, None] - idx[None, :]
    dmask = np.where(rel[None] >= 0, np.exp(log_gamma[:, None, None] * np.maximum(rel, 0)[None]), 0.0)
    q_decay = np.exp(log_gamma[None, :] * (idx + 1)[:, None])
    k_decay = np.exp(log_gamma[None, :] * (chunk - 1 - idx)[:, None])
    chunk_decay = np.exp(log_gamma * chunk)
    return dmask, q_decay, k_decay, chunk_decay


def _block_diag_gates(w_r, w_i):
    per = LRU_GROUP // LRU_BLOCK
    eye = jnp.eye(per, dtype=w_r.dtype)

    def pack(w):
        w4 = w.reshape(N_LRU_GROUPS, per, LRU_BLOCK, LRU_BLOCK)
        return jnp.einsum('gbij,bc->gbicj', w4, eye).reshape(N_LRU_GROUPS, LRU_GROUP, LRU_GROUP)

    return jnp.concatenate([pack(w_r), pack(w_i)], axis=-1).astype(BF16)


def _whole(a):
    return ([a], [_const_spec(a.shape)], None)


def _col_blocks(w, first=0, count=None):
    k, n = w.shape
    count = n // W_BLOCK - first if count is None else count
    wb = w.astype(BF16)
    specs = [pl.BlockSpec((k, W_BLOCK), functools.partial(lambda j, *_: (0, j), first + i),
                          pipeline_mode=pl.Buffered(1)) for i in range(count)]
    return ([wb] * count, specs, count)


def _flatten(groups):
    arrays = [a for g in groups for a in g[0]]
    specs = [sp for g in groups for sp in g[1]]
    return arrays, specs, [g[2] for g in groups]


def _grouped(body, counts):
    def kernel_fn(*refs):
        args, i = [], 0
        for n in counts:
            if n is None:
                args.append(refs[i])
                i += 1
            else:
                args.append(list(refs[i:i + n]))
                i += n
        assert i == len(refs)
        return body(*args)
    return kernel_fn


def _one(a, spec):
    return ([a], [spec], None)


def _call(body, name, grid, operands, out_specs, out_shape, scratch_shapes=(), out_counts=None):
    arrays, specs, counts = _flatten(operands)
    counts = counts + (out_counts or [None] * len(out_shape)) + [None] * len(scratch_shapes)
    return pl.pallas_call(
        _grouped(body, counts), grid=grid, in_specs=specs, out_specs=out_specs, out_shape=out_shape,
        scratch_shapes=list(scratch_shapes),
        compiler_params=pltpu.CompilerParams(dimension_semantics=("arbitrary",) * len(grid),
                                             vmem_limit_bytes=VMEM_LIMIT),
        name=name)(*arrays)


def kernel(x_prompt, x_sample, p_prompt, p_sample, state_lru_conv, state_lru_h, state_ret, state_ffn_conv, g_mix, w_in, w_lru_conv, b_lru_conv, w_r, b_r, w_i, b_i, lru_lambda, w_lru_out, gn_g, gn_b, w_ret_out, w_o, g_ffn, w_up, w_ffn_conv, b_ffn_conv, w_down, w_ple, g_ple, w_ple_gate, g_final):
    depth = g_mix.shape[0]
    assert depth == 1
    bp, lp, _ = x_prompt.shape
    bsq, ls, _ = x_sample.shape
    assert ls == SUBLANES and lp % T_MIX == 0 and lp % T_FFN == 0
    assert bsq % BS_RET == 0 and bsq % BS_MIX == 0
    hdv = RET_HEADS * RET_DV
    hdk = RET_HEADS * RET_DK

    mixer_weights = [_whole(_row(g_mix[0])), _col_blocks(w_in[0]), _whole(w_lru_conv[0]), _whole(_row(b_lru_conv[0])),
                     _whole(_block_diag_gates(w_r[0], w_i[0])), _whole(_row(b_r[0])), _whole(_row(b_i[0])),
                     _whole(_row(lru_lambda[0])), _col_blocks(w_lru_out[0]), _whole(_row(gn_g[0])),
                     _whole(_row(gn_b[0])), _col_blocks(w_ret_out[0]), _col_blocks(w_o[0])]
    n_steps = bp * (lp // T_MIX)
    to_cast = [w_up[0], w_down[0], w_ple_gate[0]]
    slab = lambda w: pl.BlockSpec((w.shape[0] // n_steps, w.shape[1]), lambda b, t: (b * (lp // T_MIX) + t, 0))
    assert all(w.shape[0] % n_steps == 0 and (w.shape[0] // n_steps) % BF16_ROWS == 0 for w in to_cast)

    cos_p, sin_p = _rotary_tables(0, lp)
    dmask, q_decay, k_decay, chunk_decay = _decay_tables(T_MIX)
    dmask = jnp.asarray(dmask, F32)
    qdec_p = jnp.asarray(np.repeat(q_decay, RET_DV, axis=1), F32)
    kdec_p = jnp.asarray(np.repeat(k_decay, RET_DK, axis=1), F32)
    cdec_p = jnp.asarray(np.broadcast_to(np.repeat(chunk_decay, RET_DK).reshape(N_PAIRS, 2 * RET_DK, 1),
                                         (N_PAIRS, 2 * RET_DK, RET_DV)), F32)
    tile = lambda w: pl.BlockSpec((None, T_MIX, w), lambda b, t: (b, t, 0))
    table = pl.BlockSpec((T_MIX, LANES), lambda b, t: (t, 0))
    x1_p, convst_p, hst_p, sst_p, wup_b, wdown_b, wgate_b = _call(
        _mixer_prompt_kernel, "mixer_prompt", (bp, lp // T_MIX),
        [_one(x_prompt, tile(D_MODEL)), _one(cos_p, table), _one(sin_p, table),
         (to_cast, [slab(w) for w in to_cast], len(to_cast)),
         _whole(dmask), _whole(qdec_p), _whole(kdec_p), _whole(cdec_p)] + mixer_weights,
        out_specs=[tile(D_MODEL),
                   pl.BlockSpec((None, LRU_CONV - 1, D_LRU), lambda b, t: (b, 0, 0)),
                   pl.BlockSpec((None, 1, D_LRU), lambda b, t: (b, 0, 0)),
                   pl.BlockSpec((None, N_PAIRS, 2 * RET_DK, RET_DV), lambda b, t: (b, 0, 0, 0))]
                  + [slab(w) for w in to_cast],
        out_shape=[jax.ShapeDtypeStruct((bp, lp, D_MODEL), F32),
                   jax.ShapeDtypeStruct((bp, LRU_CONV - 1, D_LRU), F32),
                   jax.ShapeDtypeStruct((bp, 1, D_LRU), F32),
                   jax.ShapeDtypeStruct((bp, N_PAIRS, 2 * RET_DK, RET_DV), F32)]
                  + [jax.ShapeDtypeStruct(w.shape, BF16) for w in to_cast],
        out_counts=[None] * 4 + [len(to_cast)],
        scratch_shapes=[pltpu.VMEM((T_MIX + 2 * SUBLANES, w_in.shape[2]), F32), pltpu.VMEM((SUBLANES, D_LRU), F32),
                        pltpu.VMEM((N_PAIRS, 2 * RET_DK, RET_DV), F32)])

    ffn_weights = [_whole(_row(g_ffn[0])), _col_blocks(wup_b), _whole(w_ffn_conv[0]), _whole(_row(b_ffn_conv[0])),
                   _col_blocks(wdown_b), _col_blocks(w_ple[0]), _whole(_row(g_ple[0])),
                   _col_blocks(wgate_b), _whole(_row(g_final))]
    tilef = lambda w: pl.BlockSpec((None, T_FFN, w), lambda b, t: (b, t, 0))
    y_p, ffnst_p = _call(
        functools.partial(_ffn_kernel, decode=False), "ffn_prompt", (bp, lp // T_FFN),
        [_one(x1_p, tilef(D_MODEL)), _one(p_prompt[0], tilef(PLE_DIM))] + ffn_weights,
        out_specs=[tilef(D_MODEL), pl.BlockSpec((None, FFN_CONV - 1, 2 * D_FF), lambda b, t: (b, 0, 0))],
        out_shape=[jax.ShapeDtypeStruct((bp, lp, D_MODEL), F32),
                   jax.ShapeDtypeStruct((bp, FFN_CONV - 1, 2 * D_FF), F32)],
        scratch_shapes=[pltpu.VMEM((4, T_FFN + 2 * SUBLANES, FFN_CHUNK), F32),
                        pltpu.VMEM((SUBLANES, 2 * D_FF), F32)])

    rows_s = bsq * ls
    cos_s, sin_s = _rotary_tables(PAST_LEN, ls)
    dmask8, q_decay8, k_decay8, _ = _decay_tables(ls)
    d64 = jnp.asarray(dmask8.reshape(RET_HEADS * ls, ls), F32)
    qdec64 = jnp.asarray(np.broadcast_to(q_decay8.T.reshape(RET_HEADS * ls, 1), (RET_HEADS * ls, RET_DV)), F32)
    kdec8 = jnp.asarray(np.repeat(k_decay8, RET_DK, axis=1), F32)
    rr = BS_RET * ls
    state_block = pl.BlockSpec((BS_RET, RET_HEADS, RET_DK, RET_DV), lambda i: (i, 0, 0, 0))
    o_s, sst_s = _call(
        _ret_sample_kernel, "ret_sample", (bsq // BS_RET,),
        [_one(x_sample.reshape(rows_s, D_MODEL), pl.BlockSpec((rr, D_MODEL), lambda i: (i, 0))),
         _one(state_ret[0], state_block),
         _whole(cos_s), _whole(sin_s), _whole(d64), _whole(qdec64), _whole(kdec8), _whole(_row(g_mix[0])),
         _col_blocks(w_in[0], BLK_Q[0], BLK_RG[0] - BLK_Q[0])],
        out_specs=[pl.BlockSpec((rr, hdv), lambda i: (i, 0)), state_block],
        out_shape=[jax.ShapeDtypeStruct((rows_s, hdv), F32),
                   jax.ShapeDtypeStruct((bsq, RET_HEADS, RET_DK, RET_DV), F32)],
        scratch_shapes=[pltpu.VMEM((rr, hdk), F32), pltpu.VMEM((rr, hdk), F32), pltpu.VMEM((rr, hdv), F32)])

    seqs = lambda *dims: pl.BlockSpec((BS_MIX,) + dims, lambda i: (i,) + (0,) * len(dims))
    by_row = lambda rows, c: pl.BlockSpec((rows, BS_MIX, c), lambda i: (0, i, 0))
    keep_l, keep_f = LRU_CONV - 1, FFN_CONV - 1
    x1_s, convst_s, hst_s = _call(
        _mixer_sample_kernel, "mixer_sample", (bsq // BS_MIX,),
        [_one(x_sample, seqs(ls, D_MODEL)), _one(o_s.reshape(bsq, ls, hdv), seqs(ls, hdv)),
         _one(state_lru_conv[0].reshape(bsq, keep_l * D_LRU), seqs(keep_l * D_LRU)),
         _one(state_lru_h[0], seqs(D_LRU))] + mixer_weights,
        out_specs=[by_row(ls, D_MODEL), by_row(keep_l, D_LRU), seqs(D_LRU)],
        out_shape=[jax.ShapeDtypeStruct((ls, bsq, D_MODEL), F32),
                   jax.ShapeDtypeStruct((keep_l, bsq, D_LRU), F32),
                   jax.ShapeDtypeStruct((bsq, D_LRU), F32)])

    y_s, ffnst_s = _call(
        functools.partial(_ffn_kernel, decode=True), "ffn_sample", (bsq // BS_MIX,),
        [_one(x1_s, by_row(ls, D_MODEL)), _one(p_sample[0], seqs(ls, PLE_DIM)),
         _one(state_ffn_conv[0].reshape(bsq, keep_f * 2 * D_FF), seqs(keep_f * 2 * D_FF))] + ffn_weights,
        out_specs=[seqs(ls, D_MODEL), by_row(keep_f, 2 * D_FF)],
        out_shape=[jax.ShapeDtypeStruct((bsq, ls, D_MODEL), F32),
                   jax.ShapeDtypeStruct((keep_f, bsq, 2 * D_FF), F32)])

    return (y_p,
            y_s,
            convst_p[None],
            hst_p.reshape(1, bp, D_LRU),
            sst_p.reshape(1, bp, RET_HEADS, RET_DK, RET_DV),
            ffnst_p[None],
            jnp.swapaxes(convst_s, 0, 1)[None],
            hst_s[None],
            sst_s[None],
            jnp.swapaxes(ffnst_s, 0, 1)[None])
```

```python
import functools
import math

import jax
import jax.numpy as jnp
import numpy as np
from jax import lax
from jax.experimental import pallas as pl
from jax.experimental.pallas import tpu as pltpu

F32 = jnp.float32
BF16 = jnp.bfloat16

D_MODEL = 1024
PLE_DIM = 256
D_LRU = 1024
LRU_BLOCK = 64
LRU_GROUP = 256
N_LRU_GROUPS = D_LRU // LRU_GROUP
LRU_CONV = 4
LRU_C = 8.0
RET_HEADS = 8
RET_DK = 64
RET_DV = 128
N_PAIRS = RET_HEADS // 2
ROPE_BASE = 10000.0
D_FF = 3072
FFN_CONV = 3
FFN_CHUNK = 512
DOWN_GROUP = 2
EPS = 1e-6
PAST_LEN = 16384
SUBLANES = 8
BF16_ROWS = 16
LANES = 128

W_BLOCK = 512
BLK_LX, BLK_LG, BLK_Q, BLK_K, BLK_V, BLK_RG, BLK_GA, BLK_GB = (0, 2), (2, 2), (4, 1), (5, 1), (6, 2), (8, 2), (10, 2), (12, 2)

SCORES_AHEAD = 2
T_MIX = 256
T_FFN = 512
RET_UNROLL = 4
BS_RET = 32
BS_MIX = 64
VMEM_LIMIT = 56 * 1024 * 1024


def _sigmoid(x):
    return 1.0 / (1.0 + jnp.exp(-x))


def _gelu(x):
    c = math.sqrt(2.0 / math.pi)
    return x * (0.5 + 0.5 * jnp.tanh(x * (c + (c * 0.044715) * (x * x))))


def _rms(x, g):
    ms = jnp.mean(x * x, axis=-1, keepdims=True)
    return x * lax.rsqrt(ms + EPS) * g


def _dot(a, b):
    return jnp.dot(a, b, preferred_element_type=F32)


def _dot_blocks(a, w_refs, blocks=None):
    first, count = blocks if blocks is not None else (0, len(w_refs))
    parts = [_dot(a, w_refs[first + i][...]) for i in range(count)]
    return parts[0] if count == 1 else jnp.concatenate(parts, axis=1)


def _dot_row_halves(a, w):
    half = a.shape[0] // 2
    return jnp.concatenate([_dot(a[:half], w), _dot(a[half:], w)], axis=0)


def _dot_nt(a, b):
    return lax.dot_general(a, b, (((1,), (1,)), ((), ())), preferred_element_type=F32)


def _dot_tn(a, b):
    return lax.dot_general(a, b, (((0,), (0,)), ((), ())), preferred_element_type=F32)


def _rotary_slab(x, cos, sin_signed):
    lane = lax.broadcasted_iota(jnp.int32, x.shape, x.ndim - 1)
    first_half = (lane & (RET_DK - 1)) < (RET_DK // 2)
    other = jnp.where(first_half,
                      pltpu.roll(x, LANES - RET_DK // 2, axis=x.ndim - 1),
                      pltpu.roll(x, RET_DK // 2, axis=x.ndim - 1))
    return x * cos + other * sin_signed


def _token_major(ref):
    return jnp.concatenate([ref[:, t, :] for t in range(ref.shape[1])], axis=0)


def _gate_preacts(xc, wri_ref):
    xcb = xc.astype(BF16)
    return [_dot(xcb[:, g * LRU_GROUP:(g + 1) * LRU_GROUP], wri_ref[g]) for g in range(N_LRU_GROUPS)]


def _lru_gates(xc, pre, br_ref, bi_ref, lam_ref):
    pre_r = [p[:, :LRU_GROUP] for p in pre]
    pre_i = [p[:, LRU_GROUP:] for p in pre]
    r = _sigmoid(jnp.concatenate(pre_r, axis=1) + br_ref[...])
    ig = _sigmoid(jnp.concatenate(pre_i, axis=1) + bi_ref[...])
    lam = lam_ref[...]
    softplus_neg_lam = jnp.maximum(-lam, 0.0) + jnp.log1p(jnp.exp(-jnp.abs(lam)))
    log_a = (-LRU_C * r) * softplus_neg_lam
    a = jnp.exp(log_a)
    u = jnp.sqrt(1.0 - a * a) * (ig * xc)
    return a, u


def _scan_within_groups(a3, u3):
    row = lax.broadcasted_iota(jnp.int32, (1,) + a3.shape[1:], 1)
    for s in (1, 2, 4):
        valid = row >= s
        a_prev = jnp.where(valid, pltpu.roll(a3, s, axis=1), 1.0)
        u_prev = jnp.where(valid, pltpu.roll(u3, s, axis=1), 0.0)
        u3 = a3 * u_prev + u3
        a3 = a3 * a_prev
    return a3, u3


def _head_norm_gate(o, rg, gng_ref, gnb_ref):
    parts = []
    for h in range(RET_HEADS):
        oh = o[:, h * RET_DV:(h + 1) * RET_DV]
        mu = jnp.mean(oh, axis=-1, keepdims=True)
        ctr = oh - mu
        var = jnp.mean(ctr * ctr, axis=-1, keepdims=True)
        parts.append(ctr * lax.rsqrt(var + EPS))
    y = jnp.concatenate(parts, axis=1) * gng_ref[...] + gnb_ref[...]
    return y * (rg * _sigmoid(rg))


def _merge_out(x, nxb, ya, yb, win_ref, wo_ref):
    ga = _dot_blocks(nxb, win_ref, BLK_GA)
    gb = _dot_blocks(nxb, win_ref, BLK_GB)
    merged = _sigmoid(ga) * ya + _sigmoid(gb) * yb
    return x + _dot_blocks(merged.astype(BF16), wo_ref)


def _mixer_prompt_kernel(x_ref, cos_ref, sin_ref, cast_in_refs, dmask_ref, qdec_ref, kdec_ref, cdec_ref,
                         gmix_ref, win_ref, wconv_ref, bconv_ref, wri_ref, br_ref, bi_ref, lam_ref,
                         wlo_ref, gng_ref, gnb_ref, wro_ref, wo_ref,
                         x1_ref, convst_ref, hst_ref, sst_ref, cast_out_refs,
                         proj_ref, hc_ref, s_ref):
    T = T_MIX
    t = pl.program_id(1)


    @pl.when(t == 0)
    def _():
        proj_ref[0:SUBLANES, :] = jnp.zeros((SUBLANES, proj_ref.shape[1]), F32)
        hc_ref[...] = jnp.zeros_like(hc_ref)
        s_ref[...] = jnp.zeros_like(s_ref)

    x = x_ref[...]
    nxb = _rms(x, gmix_ref[...]).astype(BF16)

    def project(first, count):
        for blk in range(first, first + count):
            proj_ref[SUBLANES:SUBLANES + T, blk * W_BLOCK:(blk + 1) * W_BLOCK] = _dot(nxb, win_ref[blk][...])

    project(0, BLK_Q[0])

    def proj(blocks):
        first, count = blocks
        return proj_ref[SUBLANES:SUBLANES + T, first * W_BLOCK:(first + count) * W_BLOCK]

    lx_cols = slice(BLK_LX[0] * W_BLOCK, (BLK_LX[0] + BLK_LX[1]) * W_BLOCK)
    xc = bconv_ref[...] + proj(BLK_LX) * wconv_ref[LRU_CONV - 1:LRU_CONV, :]
    for j in range(LRU_CONV - 1):
        start = SUBLANES - (LRU_CONV - 1) + j
        xc = xc + proj_ref[start:start + T, lx_cols] * wconv_ref[j:j + 1, :]
    pre = _gate_preacts(xc, wri_ref)
    project(BLK_Q[0], len(win_ref) - BLK_Q[0])
    a, u = _lru_gates(xc, pre, br_ref, bi_ref, lam_ref)
    G = T // SUBLANES
    A3, B3 = _scan_within_groups(a.reshape(G, SUBLANES, D_LRU), u.reshape(G, SUBLANES, D_LRU))
    h_in = hc_ref[...]
    hs_groups = []
    for g in range(G):
        h = A3[g] * h_in + B3[g]
        hs_groups.append(h)
        h_in = jnp.broadcast_to(h[SUBLANES - 1:SUBLANES, :], (SUBLANES, D_LRU))
    h_last = h_in
    hc_ref[...] = h_last
    hs = jnp.concatenate(hs_groups, axis=0)
    ya = _dot_blocks((hs * _gelu(proj(BLK_LG))).astype(BF16), wlo_ref)

    for src, dst in zip(cast_in_refs, cast_out_refs):
        dst[...] = src[...].astype(BF16)

    q = proj(BLK_Q)
    k = proj(BLK_K)
    v = proj(BLK_V)
    cos = cos_ref[...]
    sin_signed = sin_ref[...]
    lane = lax.broadcasted_iota(jnp.int32, (T, LANES), 1)
    heads, kds = [], []
    for j in range(N_PAIRS):
        sl = slice(j * LANES, (j + 1) * LANES)
        qr = _rotary_slab(q[:, sl], cos, sin_signed)
        kr = _rotary_slab(k[:, sl], cos, sin_signed) * (RET_DK ** -0.5)
        kb = kr.astype(BF16)
        kds.append((kr * kdec_ref[:, sl]).astype(BF16))
        for half in range(2):
            heads.append((jnp.where((lane >> 6) == half, qr, 0.0).astype(BF16), kb))

    def masked_scores(h):
        qm, kb = heads[h]
        return (_dot_nt(qm, kb) * dmask_ref[h]).astype(BF16)

    o_parts = []
    ahead = {h: masked_scores(h) for h in range(SCORES_AHEAD)}
    for h in range(RET_HEADS):
        j, half = divmod(h, 2)
        if h + SCORES_AHEAD < RET_HEADS:
            ahead[h + SCORES_AHEAD] = masked_scores(h + SCORES_AHEAD)
        o_state = _dot(heads[h][0], s_ref[j].astype(BF16)) * qdec_ref[:, h * RET_DV:(h + 1) * RET_DV]
        o_parts.append(_dot(ahead.pop(h), v[:, h * RET_DV:(h + 1) * RET_DV].astype(BF16)) + o_state)
        if half == 1:
            upd = _dot_tn(kds[j], v[:, 2 * j * RET_DV:(2 * j + 2) * RET_DV].astype(BF16))
            s_ref[j] = s_ref[j] * cdec_ref[j] + jnp.concatenate(
                [upd[0:RET_DK, 0:RET_DV], upd[RET_DK:2 * RET_DK, RET_DV:2 * RET_DV]], axis=0)
    o = jnp.concatenate(o_parts, axis=1)
    yb = _dot_blocks(_head_norm_gate(o, proj(BLK_RG), gng_ref, gnb_ref).astype(BF16), wro_ref)

    merged = _sigmoid(proj(BLK_GA)) * ya + _sigmoid(proj(BLK_GB)) * yb
    x1_ref[...] = x + _dot_blocks(merged.astype(BF16), wo_ref)

    proj_ref[0:SUBLANES, lx_cols] = proj_ref[T:T + SUBLANES, lx_cols]

    @pl.when(t == pl.num_programs(1) - 1)
    def _():
        convst_ref[...] = proj_ref[SUBLANES - (LRU_CONV - 1):SUBLANES, lx_cols]
        hst_ref[...] = h_last[0:1, :]
        sst_ref[...] = s_ref[...]


def _ret_sample_kernel(x_ref, st_ref, cos_ref, sin_ref, d64_ref, qdec_ref, kdec_ref,
                       gmix_ref, wqkv_ref,
                       o_ref, so_ref,
                       q_ref, k_ref, v_ref):
    bs = BS_RET
    nq = RET_HEADS * RET_DK
    nxb = _rms(x_ref[...], gmix_ref[...]).astype(BF16)
    qkv = _dot_blocks(nxb, wqkv_ref)
    cos = cos_ref[...][None]
    sin_signed = sin_ref[...][None]
    for j in range(N_PAIRS):
        sl = slice(j * LANES, (j + 1) * LANES)
        q3 = qkv[:, j * LANES:(j + 1) * LANES].reshape(bs, SUBLANES, LANES)
        k3 = qkv[:, nq + j * LANES:nq + (j + 1) * LANES].reshape(bs, SUBLANES, LANES)
        q_ref[:, sl] = _rotary_slab(q3, cos, sin_signed).reshape(bs * SUBLANES, LANES)
        k_ref[:, sl] = (_rotary_slab(k3, cos, sin_signed) * (RET_DK ** -0.5)).reshape(bs * SUBLANES, LANES)
    v_ref[...] = qkv[:, 2 * nq:]

    lane = lax.broadcasted_iota(jnp.int32, (SUBLANES, nq), 1)
    gammas = [1.0 - 2.0 ** (-5.0 - h) for h in range(RET_HEADS)]

    def first_stage(s):
        r0 = pl.multiple_of(s * SUBLANES, SUBLANES)
        qs = q_ref[pl.ds(r0, SUBLANES), :]
        ks = k_ref[pl.ds(r0, SUBLANES), :]
        qbd = jnp.concatenate([jnp.where((lane >> 6) == h, qs, 0.0) for h in range(RET_HEADS)],
                              axis=0).astype(BF16)
        o_state = _dot(qbd, st_ref[s].reshape(RET_HEADS * RET_DK, RET_DV).astype(BF16)) * qdec_ref[...]
        p = (_dot_nt(qbd, ks.astype(BF16)) * d64_ref[...]).astype(BF16)
        kd = (ks * kdec_ref[...]).astype(BF16)
        return o_state, p, kd

    def second_stage(s, o_state, p, kd):
        r0 = pl.multiple_of(s * SUBLANES, SUBLANES)
        vb = v_ref[pl.ds(r0, SUBLANES), :].astype(BF16)
        s0 = st_ref[s]
        o_intra = _dot(p, vb)
        o_ref[pl.ds(r0, SUBLANES), :] = jnp.concatenate(
            [o_intra[h * SUBLANES:(h + 1) * SUBLANES, h * RET_DV:(h + 1) * RET_DV]
             + o_state[h * SUBLANES:(h + 1) * SUBLANES, :] for h in range(RET_HEADS)], axis=1)
        for j in range(N_PAIRS):
            upd = _dot_tn(kd[:, j * LANES:(j + 1) * LANES], vb[:, 2 * j * RET_DV:(2 * j + 2) * RET_DV])
            so_ref[s, 2 * j] = s0[2 * j] * (gammas[2 * j] ** SUBLANES) + upd[0:RET_DK, 0:RET_DV]
            so_ref[s, 2 * j + 1] = (s0[2 * j + 1] * (gammas[2 * j + 1] ** SUBLANES)
                                    + upd[RET_DK:2 * RET_DK, RET_DV:2 * RET_DV])

    def per_seq(s, carry):
        nxt = first_stage(jnp.minimum(s + 1, bs - 1))
        second_stage(s, *carry)
        return nxt

    lax.fori_loop(0, bs, per_seq, first_stage(0), unroll=RET_UNROLL)


def _mixer_sample_kernel(x_ref, o_ref, cst_ref, h0_ref,
                         gmix_ref, win_ref, wconv_ref, bconv_ref, wri_ref, br_ref, bi_ref, lam_ref,
                         wlo_ref, gng_ref, gnb_ref, wro_ref, wo_ref,
                         x1_ref, convst_ref, hst_ref):
    bs, steps = x_ref.shape[0], x_ref.shape[1]
    x = _token_major(x_ref)
    nxb = _rms(x, gmix_ref[...]).astype(BF16)

    lx = _dot_blocks(nxb, win_ref, BLK_LX)
    keep = LRU_CONV - 1
    ext = ([cst_ref[:, j * D_LRU:(j + 1) * D_LRU] for j in range(keep)]
           + [lx[t * bs:(t + 1) * bs] for t in range(steps)])
    xc = jnp.concatenate(
        [bconv_ref[...] + sum(ext[t + j] * wconv_ref[j:j + 1, :] for j in range(LRU_CONV)) for t in range(steps)],
        axis=0)
    for j in range(keep):
        convst_ref[j] = ext[steps + j]
    a, u = _lru_gates(xc, _gate_preacts(xc, wri_ref), br_ref, bi_ref, lam_ref)
    h = h0_ref[...]
    hs = []
    for t in range(steps):
        h = a[t * bs:(t + 1) * bs] * h + u[t * bs:(t + 1) * bs]
        hs.append(h)
    hst_ref[...] = h
    lg = _dot_blocks(nxb, win_ref, BLK_LG)
    ya = _dot_blocks((jnp.concatenate(hs, axis=0) * _gelu(lg)).astype(BF16), wlo_ref)

    rg = _dot_blocks(nxb, win_ref, BLK_RG)
    o = _token_major(o_ref)
    yb = _dot_blocks(_head_norm_gate(o, rg, gng_ref, gnb_ref).astype(BF16), wro_ref)

    x1_ref[...] = _merge_out(x, nxb, ya, yb, win_ref, wo_ref).reshape(steps, bs, D_MODEL)


def _ffn_kernel(x_ref, p_ref, *rest, decode):
    keep = FFN_CONV - 1
    if decode:
        (cst_ref, gffn_ref, wup_ref, wconv_ref, bconv_ref, wdown_ref, wple_ref, gple_ref, wgate_ref, gfin_ref,
         y_ref, convst_ref) = rest
        steps, bs = x_ref.shape[0], x_ref.shape[1]
        x = x_ref[...].reshape(steps * bs, D_MODEL)
        p = _token_major(p_ref)
    else:
        (gffn_ref, wup_ref, wconv_ref, bconv_ref, wdown_ref, wple_ref, gple_ref, wgate_ref, gfin_ref,
         y_ref, convst_ref, z_ref, carry_ref) = rest
        t = pl.program_id(1)

        @pl.when(t == 0)
        def _():
            carry_ref[...] = jnp.zeros_like(carry_ref)

        x = x_ref[...]
        p = p_ref[...]
    rows = x.shape[0]
    nxb = _rms(x, gffn_ref[...]).astype(BF16)
    n_chunks = D_FF // FFN_CHUNK
    act_chunks = []
    acc = None
    for c in range(n_chunks):
        halves = []
        for half in range(2):
            blk = half * n_chunks + c
            cols = slice(blk * FFN_CHUNK, (blk + 1) * FFN_CHUNK)
            up = (_dot_row_halves if c == 0 and half == 0 else _dot)(nxb, wup_ref[blk][...])
            w = wconv_ref[:, cols]
            bias = bconv_ref[:, cols]
            if decode:
                ext = ([cst_ref[:, j * 2 * D_FF + cols.start:j * 2 * D_FF + cols.stop] for j in range(keep)]
                       + [up[s * bs:(s + 1) * bs] for s in range(steps)])
                halves.append(jnp.concatenate(
                    [bias + sum(ext[s + j] * w[j:j + 1] for j in range(FFN_CONV)) for s in range(steps)], axis=0))
                for j in range(keep):
                    convst_ref[j, :, cols] = ext[steps + j]
            else:
                zc_ref = z_ref.at[2 * (c % 2) + half]
                zc_ref[0:SUBLANES, :] = carry_ref[:, cols]
                zc_ref[SUBLANES:SUBLANES + rows, :] = up
                y = bias + up * w[keep:keep + 1]
                for j in range(keep):
                    start = SUBLANES - keep + j
                    y = y + zc_ref[start:start + rows, :] * w[j:j + 1]
                carry_ref[:, cols] = zc_ref[rows:rows + SUBLANES, :]
                halves.append(y)
        act_chunks.append((_gelu(halves[0]) * halves[1]).astype(BF16))
        if len(act_chunks) == DOWN_GROUP:
            k0 = (c + 1 - DOWN_GROUP) * FFN_CHUNK
            part = jnp.concatenate(
                [_dot(jnp.concatenate(act_chunks, axis=1), w[k0:k0 + DOWN_GROUP * FFN_CHUNK, :]) for w in wdown_ref],
                axis=1)
            acc = part if acc is None else acc + part
            act_chunks = []
    x2 = x + acc

    e = _rms(_dot_blocks(p.astype(BF16), wple_ref), gple_ref[...])
    x2b = x2.astype(BF16)
    x3 = x2 + _sigmoid(jnp.concatenate([_dot_row_halves(x2b, w[...]) for w in wgate_ref], axis=1)) * e
    y = _rms(x3, gfin_ref[...])

    if decode:
        for s in range(steps):
            y_ref[:, s, :] = y[s * bs:(s + 1) * bs]
    else:
        y_ref[...] = y

        @pl.when(t == pl.num_programs(1) - 1)
        def _():
            convst_ref[...] = carry_ref[SUBLANES - keep:SUBLANES, :]


def _const_spec(shape):
    n = len(shape)
    return pl.BlockSpec(shape, lambda *_: (0,) * n, pipeline_mode=pl.Buffered(1))


def _row(v):
    return v.reshape(1, -1)


def _rotary_tables(first_pos, n):
    half = RET_DK // 2
    inv = ROPE_BASE ** (-np.arange(half, dtype=np.float64) / half)
    ang = (first_pos + np.arange(n, dtype=np.float64))[:, None] * inv[None, :]
    cos, sin = np.cos(ang), np.sin(ang)
    cos_t = np.tile(cos, (1, LANES // half))
    sin_t = np.tile(np.concatenate([-sin, sin], axis=1), (1, LANES // RET_DK))
    return jnp.asarray(cos_t, F32), jnp.asarray(sin_t, F32)


def _decay_tables(chunk):
    log_gamma = np.log1p(-(2.0 ** (-5.0 - np.arange(RET_HEADS, dtype=np.float64))))
    idx = np.arange(chunk)
    rel = idx[:, None] - idx[None, :]
    dmask = np.where(rel[None] >= 0, np.exp(log_gamma[:, None, None] * np.maximum(rel, 0)[None]), 0.0)
    q_decay = np.exp(log_gamma[None, :] * (idx + 1)[:, None])
    k_decay = np.exp(log_gamma[None, :] * (chunk - 1 - idx)[:, None])
    chunk_decay = np.exp(log_gamma * chunk)
    return dmask, q_decay, k_decay, chunk_decay


def _block_diag_gates(w_r, w_i):
    per = LRU_GROUP // LRU_BLOCK
    eye = jnp.eye(per, dtype=w_r.dtype)

    def pack(w):
        w4 = w.reshape(N_LRU_GROUPS, per, LRU_BLOCK, LRU_BLOCK)
        return jnp.einsum('gbij,bc->gbicj', w4, eye).reshape(N_LRU_GROUPS, LRU_GROUP, LRU_GROUP)

    return jnp.concatenate([pack(w_r), pack(w_i)], axis=-1).astype(BF16)


def _whole(a):
    return ([a], [_const_spec(a.shape)], None)


def _col_blocks(w, first=0, count=None):
    k, n = w.shape
    count = n // W_BLOCK - first if count is None else count
    wb = w.astype(BF16)
    specs = [pl.BlockSpec((k, W_BLOCK), functools.partial(lambda j, *_: (0, j), first + i),
                          pipeline_mode=pl.Buffered(1)) for i in range(count)]
    return ([wb] * count, specs, count)


def _flatten(groups):
    arrays = [a for g in groups for a in g[0]]
    specs = [sp for g in groups for sp in g[1]]
    return arrays, specs, [g[2] for g in groups]


def _grouped(body, counts):
    def kernel_fn(*refs):
        args, i = [], 0
        for n in counts:
            if n is None:
                args.append(refs[i])
                i += 1
            else:
                args.append(list(refs[i:i + n]))
                i += n
        assert i == len(refs)
        return body(*args)
    return kernel_fn


def _one(a, spec):
    return ([a], [spec], None)


def _call(body, name, grid, operands, out_specs, out_shape, scratch_shapes=(), out_counts=None):
    arrays, specs, counts = _flatten(operands)
    counts = counts + (out_counts or [None] * len(out_shape)) + [None] * len(scratch_shapes)
    return pl.pallas_call(
        _grouped(body, counts), grid=grid, in_specs=specs, out_specs=out_specs, out_shape=out_shape,
        scratch_shapes=list(scratch_shapes),
        compiler_params=pltpu.CompilerParams(dimension_semantics=("arbitrary",) * len(grid),
                                             vmem_limit_bytes=VMEM_LIMIT),
        name=name)(*arrays)


def kernel(x_prompt, x_sample, p_prompt, p_sample, state_lru_conv, state_lru_h, state_ret, state_ffn_conv, g_mix, w_in, w_lru_conv, b_lru_conv, w_r, b_r, w_i, b_i, lru_lambda, w_lru_out, gn_g, gn_b, w_ret_out, w_o, g_ffn, w_up, w_ffn_conv, b_ffn_conv, w_down, w_ple, g_ple, w_ple_gate, g_final):
    depth = g_mix.shape[0]
    assert depth == 1
    bp, lp, _ = x_prompt.shape
    bsq, ls, _ = x_sample.shape
    assert ls == SUBLANES and lp % T_MIX == 0 and lp % T_FFN == 0
    assert bsq % BS_RET == 0 and bsq % BS_MIX == 0
    hdv = RET_HEADS * RET_DV
    hdk = RET_HEADS * RET_DK

    mixer_weights = [_whole(_row(g_mix[0])), _col_blocks(w_in[0]), _whole(w_lru_conv[0]), _whole(_row(b_lru_conv[0])),
                     _whole(_block_diag_gates(w_r[0], w_i[0])), _whole(_row(b_r[0])), _whole(_row(b_i[0])),
                     _whole(_row(lru_lambda[0])), _col_blocks(w_lru_out[0]), _whole(_row(gn_g[0])),
                     _whole(_row(gn_b[0])), _col_blocks(w_ret_out[0]), _col_blocks(w_o[0])]
    n_steps = bp * (lp // T_MIX)
    to_cast = [w_up[0], w_down[0], w_ple_gate[0]]
    slab = lambda w: pl.BlockSpec((w.shape[0] // n_steps, w.shape[1]), lambda b, t: (b * (lp // T_MIX) + t, 0))
    assert all(w.shape[0] % n_steps == 0 and (w.shape[0] // n_steps) % BF16_ROWS == 0 for w in to_cast)

    cos_p, sin_p = _rotary_tables(0, lp)
    dmask, q_decay, k_decay, chunk_decay = _decay_tables(T_MIX)
    dmask = jnp.asarray(dmask, F32)
    qdec_p = jnp.asarray(np.repeat(q_decay, RET_DV, axis=1), F32)
    kdec_p = jnp.asarray(np.repeat(k_decay, RET_DK, axis=1), F32)
    cdec_p = jnp.asarray(np.broadcast_to(np.repeat(chunk_decay, RET_DK).reshape(N_PAIRS, 2 * RET_DK, 1),
                                         (N_PAIRS, 2 * RET_DK, RET_DV)), F32)
    tile = lambda w: pl.BlockSpec((None, T_MIX, w), lambda b, t: (b, t, 0))
    table = pl.BlockSpec((T_MIX, LANES), lambda b, t: (t, 0))
    x1_p, convst_p, hst_p, sst_p, wup_b, wdown_b, wgate_b = _call(
        _mixer_prompt_kernel, "mixer_prompt", (bp, lp // T_MIX),
        [_one(x_prompt, tile(D_MODEL)), _one(cos_p, table), _one(sin_p, table),
         (to_cast, [slab(w) for w in to_cast], len(to_cast)),
         _whole(dmask), _whole(qdec_p), _whole(kdec_p), _whole(cdec_p)] + mixer_weights,
        out_specs=[tile(D_MODEL),
                   pl.BlockSpec((None, LRU_CONV - 1, D_LRU), lambda b, t: (b, 0, 0)),
                   pl.BlockSpec((None, 1, D_LRU), lambda b, t: (b, 0, 0)),
                   pl.BlockSpec((None, N_PAIRS, 2 * RET_DK, RET_DV), lambda b, t: (b, 0, 0, 0))]
                  + [slab(w) for w in to_cast],
        out_shape=[jax.ShapeDtypeStruct((bp, lp, D_MODEL), F32),
                   jax.ShapeDtypeStruct((bp, LRU_CONV - 1, D_LRU), F32),
                   jax.ShapeDtypeStruct((bp, 1, D_LRU), F32),
                   jax.ShapeDtypeStruct((bp, N_PAIRS, 2 * RET_DK, RET_DV), F32)]
                  + [jax.ShapeDtypeStruct(w.shape, BF16) for w in to_cast],
        out_counts=[None] * 4 + [len(to_cast)],
        scratch_shapes=[pltpu.VMEM((T_MIX + 2 * SUBLANES, w_in.shape[2]), F32), pltpu.VMEM((SUBLANES, D_LRU), F32),
                        pltpu.VMEM((N_PAIRS, 2 * RET_DK, RET_DV), F32)])

    ffn_weights = [_whole(_row(g_ffn[0])), _col_blocks(wup_b), _whole(w_ffn_conv[0]), _whole(_row(b_ffn_conv[0])),
                   _col_blocks(wdown_b), _col_blocks(w_ple[0]), _whole(_row(g_ple[0])),
                   _col_blocks(wgate_b), _whole(_row(g_final))]
    tilef = lambda w: pl.BlockSpec((None, T_FFN, w), lambda b, t: (b, t, 0))
    y_p, ffnst_p = _call(
        functools.partial(_ffn_kernel, decode=False), "ffn_prompt", (bp, lp // T_FFN),
        [_one(x1_p, tilef(D_MODEL)), _one(p_prompt[0], tilef(PLE_DIM))] + ffn_weights,
        out_specs=[tilef(D_MODEL), pl.BlockSpec((None, FFN_CONV - 1, 2 * D_FF), lambda b, t: (b, 0, 0))],
        out_shape=[jax.ShapeDtypeStruct((bp, lp, D_MODEL), F32),
                   jax.ShapeDtypeStruct((bp, FFN_CONV - 1, 2 * D_FF), F32)],
        scratch_shapes=[pltpu.VMEM((4, T_FFN + 2 * SUBLANES, FFN_CHUNK), F32),
                        pltpu.VMEM((SUBLANES, 2 * D_FF), F32)])

    rows_s = bsq * ls
    cos_s, sin_s = _rotary_tables(PAST_LEN, ls)
    dmask8, q_decay8, k_decay8, _ = _decay_tables(ls)
    d64 = jnp.asarray(dmask8.reshape(RET_HEADS * ls, ls), F32)
    qdec64 = jnp.asarray(np.broadcast_to(q_decay8.T.reshape(RET_HEADS * ls, 1), (RET_HEADS * ls, RET_DV)), F32)
    kdec8 = jnp.asarray(np.repeat(k_decay8, RET_DK, axis=1), F32)
    rr = BS_RET * ls
    state_block = pl.BlockSpec((BS_RET, RET_HEADS, RET_DK, RET_DV), lambda i: (i, 0, 0, 0))
    o_s, sst_s = _call(
        _ret_sample_kernel, "ret_sample", (bsq // BS_RET,),
        [_one(x_sample.reshape(rows_s, D_MODEL), pl.BlockSpec((rr, D_MODEL), lambda i: (i, 0))),
         _one(state_ret[0], state_block),
         _whole(cos_s), _whole(sin_s), _whole(d64), _whole(qdec64), _whole(kdec8), _whole(_row(g_mix[0])),
         _col_blocks(w_in[0], BLK_Q[0], BLK_RG[0] - BLK_Q[0])],
        out_specs=[pl.BlockSpec((rr, hdv), lambda i: (i, 0)), state_block],
        out_shape=[jax.ShapeDtypeStruct((rows_s, hdv), F32),
                   jax.ShapeDtypeStruct((bsq, RET_HEADS, RET_DK, RET_DV), F32)],
        scratch_shapes=[pltpu.VMEM((rr, hdk), F32), pltpu.VMEM((rr, hdk), F32), pltpu.VMEM((rr, hdv), F32)])

    seqs = lambda *dims: pl.BlockSpec((BS_MIX,) + dims, lambda i: (i,) + (0,) * len(dims))
    by_row = lambda rows, c: pl.BlockSpec((rows, BS_MIX, c), lambda i: (0, i, 0))
    keep_l, keep_f = LRU_CONV - 1, FFN_CONV - 1
    x1_s, convst_s, hst_s = _call(
        _mixer_sample_kernel, "mixer_sample", (bsq // BS_MIX,),
        [_one(x_sample, seqs(ls, D_MODEL)), _one(o_s.reshape(bsq, ls, hdv), seqs(ls, hdv)),
         _one(state_lru_conv[0].reshape(bsq, keep_l * D_LRU), seqs(keep_l * D_LRU)),
         _one(state_lru_h[0], seqs(D_LRU))] + mixer_weights,
        out_specs=[by_row(ls, D_MODEL), by_row(keep_l, D_LRU), seqs(D_LRU)],
        out_shape=[jax.ShapeDtypeStruct((ls, bsq, D_MODEL), F32),
                   jax.ShapeDtypeStruct((keep_l, bsq, D_LRU), F32),
                   jax.ShapeDtypeStruct((bsq, D_LRU), F32)])

    y_s, ffnst_s = _call(
        functools.partial(_ffn_kernel, decode=True), "ffn_sample", (bsq // BS_MIX,),
        [_one(x1_s, by_row(ls, D_MODEL)), _one(p_sample[0], seqs(ls, PLE_DIM)),
         _one(state_ffn_conv[0].reshape(bsq, keep_f * 2 * D_FF), seqs(keep_f * 2 * D_FF))] + ffn_weights,
        out_specs=[seqs(ls, D_MODEL), by_row(keep_f, 2 * D_FF)],
        out_shape=[jax.ShapeDtypeStruct((bsq, ls, D_MODEL), F32),
                   jax.ShapeDtypeStruct((keep_f, bsq, 2 * D_FF), F32)])

    return (y_p,
            y_s,
            convst_p[None],
            hst_p.reshape(1, bp, D_LRU),
            sst_p.reshape(1, bp, RET_HEADS, RET_DK, RET_DV),
            ffnst_p[None],
            jnp.swapaxes(convst_s, 0, 1)[None],
            hst_s[None],
            sst_s[None],
            jnp.swapaxes(ffnst_s, 0, 1)[None])
```
